```python
import math
import jax, jax.numpy as jnp
from jax import lax
import numpy as np

D_MODEL = 1024
BATCH = 32
SEQ = 256
DEPTH = 4
DEC_BATCH = 4
DEC_SEQ = 4096
PAST_LEN = 256

GRID_W = 64
N_MOD = 6
RMS_EPS = 1e-6
SHORT_CONV = 3
HY_WIDTH = 384
HY_EMB = 33
HY_FILTER_FF = 64
HY_DECAY_TARGET = 1e-2
HY_FAST_DECAY_PCT = 0.3
HY_SLOW_DECAY_PCT = 1.5
FN_GROUPS = 4
FN_GROUP_DIM = 64
FN_WIDTH = FN_GROUPS * FN_GROUP_DIM
NA_HEADS = 6
NA_HEAD_DIM = 64
NA_WIDTH = NA_HEADS * NA_HEAD_DIM
NA_WIN_ROWS = 8
NA_WIN_COLS = 16
ATTN_SCALE = NA_HEAD_DIM ** -0.5
N_BRANCHES = 3
IN_WIDTH = 3 * HY_WIDTH + FN_WIDTH + 3 * NA_WIDTH + N_BRANCHES * D_MODEL
Q_BLOCK = 128
N_EXPERTS = 32
TOP_K = 4
EXPERT_FF = D_MODEL
SWIGLU_LIMIT = 7.0
SWIGLU_ALPHA = 1.702
MOE_BLOCK = 128

kernel_name = "hyena_fnet_natten_moe_prefix_dit"


def rms_norm(x, g):
    xf = x.astype(jnp.float32)
    y = xf * lax.rsqrt(jnp.mean(xf * xf, axis=-1, keepdims=True) + RMS_EPS)
    return (y * g.astype(jnp.float32)).astype(x.dtype)


def short_conv(u, w, b):
    pad = SHORT_CONV // 2
    y = lax.conv_general_dilated(u, w, window_strides=(1,), padding=((pad, pad),),
                                 dimension_numbers=("NWC", "WIO", "NWC"),
                                 feature_group_count=u.shape[-1])
    return y + b


def hyena_filters(L, w1, b1, w2, b2, w3, b3, freq, decay):
    f32 = jnp.float32
    t = jnp.linspace(0.0, 1.0, L, dtype=f32)[:, None]
    bands = (HY_EMB - 1) // 2
    omega = 2.0 * math.pi * jnp.arange(L, dtype=f32)[:, None] / L
    f = jnp.linspace(1e-4, bands - 1, bands, dtype=f32)[None, :]
    z = jnp.concatenate([t, jnp.cos(f * omega), -jnp.sin(f * omega)], axis=-1)
    fr = freq.astype(f32)
    a = jnp.sin(fr * (z @ w1.astype(f32) + b1.astype(f32)))
    a = jnp.sin(fr * (a @ w2.astype(f32) + b2.astype(f32)))
    filt = (a @ w3.astype(f32) + b3.astype(f32)).reshape(L, 2, HY_WIDTH)
    window = jnp.exp(-t * jnp.abs(decay.astype(f32)))
    filt = filt * window[:, None, :]
    filt = filt / (jnp.sum(jnp.abs(filt), axis=(0, 1), keepdims=True) + 1e-6)
    return jnp.concatenate([filt[:, 0], filt[::-1, 1]], axis=0)


def long_conv(u, filt2):
    L = u.shape[1]
    uf = jnp.fft.rfft(u.astype(jnp.float32), n=2 * L, axis=1)
    ff = jnp.fft.rfft(filt2, axis=0)
    y = jnp.fft.irfft(uf * ff[None], n=2 * L, axis=1)[:, :L]
    return y.astype(u.dtype)


def hyena_branch(u, conv_w, conv_b, w1, b1, w2, b2, w3, b3, freq, decay, skip):
    L = u.shape[1]
    uc = short_conv(u, conv_w, conv_b)
    x0, x1, v = jnp.split(uc, 3, axis=-1)
    filt2 = hyena_filters(L, w1, b1, w2, b2, w3, b3, freq, decay)
    zz = x1 * v
    return x0 * (long_conv(zz, filt2) + zz * skip)


def fnet_branch(u):
    B, L, _ = u.shape
    ug = u.astype(jnp.float32).reshape(B, L, FN_GROUPS, FN_GROUP_DIM)
    y = jnp.real(jnp.fft.fft2(ug, axes=(1, 3), norm="ortho"))
    return y.reshape(B, L, FN_WIDTH).astype(u.dtype)


def context_attention(q, k, v):
    B, L, H, Dh = q.shape
    nb = L // Q_BLOCK
    qb = q.reshape(B, nb, Q_BLOCK, H, Dh).transpose(1, 0, 2, 3, 4)

    def block(qi):
        s = jnp.einsum("bqhd,bkhd->bhqk", qi, k, preferred_element_type=jnp.float32) * ATTN_SCALE
        p = jax.nn.softmax(s, axis=-1)
        return jnp.einsum("bhqk,bkhd->bqhd", p.astype(v.dtype), v)

    o = lax.map(block, qb)
    return o.transpose(1, 0, 2, 3, 4).reshape(B, L, H * Dh)


def neighbourhood_attention(q, k, v, k_ctx, v_ctx, rel_bias):
    B, L, H, Dh = q.shape
    rows = L // GRID_W
    wr = min(NA_WIN_ROWS, rows)
    qg = q.reshape(B, rows, GRID_W, H, Dh)
    kg = k.reshape(B, rows, GRID_W, H, Dh)
    vg = v.reshape(B, rows, GRID_W, H, Dh)
    cols = jnp.arange(GRID_W)
    cs = jnp.clip(cols - NA_WIN_COLS // 2, 0, GRID_W - NA_WIN_COLS)
    col_mask = (cols[None, :] >= cs[:, None]) & (cols[None, :] < cs[:, None] + NA_WIN_COLS)
    dc_idx = jnp.clip(cols[None, :] - cols[:, None], -(NA_WIN_COLS - 1), NA_WIN_COLS - 1) + NA_WIN_COLS - 1

    def row_block(r):
        rs = jnp.clip(r - wr // 2, 0, rows - wr)
        q_r = lax.dynamic_index_in_dim(qg, r, axis=1, keepdims=False)
        k_b = lax.dynamic_slice_in_dim(kg, rs, wr, axis=1)
        v_b = lax.dynamic_slice_in_dim(vg, rs, wr, axis=1)
        dr_idx = rs + jnp.arange(wr) - r + NA_WIN_ROWS - 1
        bias = rel_bias[:, dr_idx[None, :, None], dc_idx[:, None, :]]
        s_loc = jnp.einsum("bqhd,bukhd->bhquk", q_r, k_b,
                           preferred_element_type=jnp.float32) * ATTN_SCALE
        s_loc = jnp.where(col_mask[:, None, :], s_loc + bias.astype(jnp.float32)[None], -jnp.inf)
        s_loc = s_loc.reshape(B, H, GRID_W, wr * GRID_W)
        s_ctx = jnp.einsum("bqhd,bshd->bhqs", q_r, k_ctx,
                           preferred_element_type=jnp.float32) * ATTN_SCALE
        p = jax.nn.softmax(jnp.concatenate([s_loc, s_ctx], axis=-1), axis=-1)
        p_loc = p[..., :wr * GRID_W].reshape(B, H, GRID_W, wr, GRID_W).astype(v.dtype)
        p_ctx = p[..., wr * GRID_W:].astype(v.dtype)
        return (jnp.einsum("bhquk,bukhd->bqhd", p_loc, v_b)
                + jnp.einsum("bhqs,bshd->bqhd", p_ctx, v_ctx))

    o = lax.map(row_block, jnp.arange(rows))
    return o.transpose(1, 0, 2, 3, 4).reshape(B, L, H * Dh)


def moe_ffn(h, router_w, router_b, w1, b1, w2, b2):
    B, L, D = h.shape
    T = B * L
    xt = h.reshape(T, D)
    logits = (xt @ router_w + router_b).astype(jnp.float32)
    top_val, top_idx = lax.top_k(logits, TOP_K)
    gate = jax.nn.softmax(top_val, axis=-1)
    n_assign = T * TOP_K
    flat_e = top_idx.reshape(n_assign)
    order = jnp.argsort(flat_e)
    sorted_e = flat_e[order]
    sorted_tok = order // TOP_K
    sorted_gate = gate.reshape(n_assign)[order]
    counts = jnp.bincount(flat_e, length=N_EXPERTS)
    blocks_per_e = (counts + MOE_BLOCK - 1) // MOE_BLOCK
    block_end = jnp.cumsum(blocks_per_e)
    block_start = block_end - blocks_per_e
    assign_start = jnp.cumsum(counts) - counts
    rank = jnp.arange(n_assign) - assign_start[sorted_e]
    slot = block_start[sorted_e] * MOE_BLOCK + rank
    n_blocks = -(-n_assign // MOE_BLOCK) + N_EXPERTS
    slot_tok = jnp.full((n_blocks * MOE_BLOCK,), T, jnp.int32).at[slot].set(sorted_tok)
    block_expert = jnp.minimum(jnp.searchsorted(block_end, jnp.arange(n_blocks), side="right"),
                               N_EXPERTS - 1)
    x_pad = jnp.concatenate([xt, jnp.zeros((1, D), xt.dtype)], axis=0)
    xb = x_pad[slot_tok].reshape(n_blocks, MOE_BLOCK, D)

    def expert_block(args):
        xi, e = args
        a = xi @ w1[e] + b1[e]
        x_glu = jnp.minimum(a[:, 0::2], SWIGLU_LIMIT)
        x_lin = jnp.clip(a[:, 1::2], -SWIGLU_LIMIT, SWIGLU_LIMIT)
        act = x_glu * jax.nn.sigmoid(SWIGLU_ALPHA * x_glu) * (x_lin + 1)
        return act @ w2[e] + b2[e]

    yb = lax.map(expert_block, (xb, block_expert)).reshape(n_blocks * MOE_BLOCK, D)
    y_assign = (yb[slot] * sorted_gate[:, None]).astype(xt.dtype)
    out = jnp.zeros((T, D), xt.dtype).at[sorted_tok].add(y_assign)
    return out.reshape(B, L, D)


def trunk_layer(x, cond, p, k_ctx=None, v_ctx=None):
    B, L, _ = x.shape
    mod = (jax.nn.silu(cond) @ p["ada_w"] + p["ada_b"])[:, None, :]
    sh1, sc1, g1, sh2, sc2, g2 = jnp.split(mod, N_MOD, axis=-1)
    h = rms_norm(x, p["norm_mix"]) * (1 + sc1) + sh1
    u = h @ p["w_in"]
    o1 = 3 * HY_WIDTH
    o2 = o1 + FN_WIDTH
    o3 = o2 + NA_WIDTH
    o4 = o3 + NA_WIDTH
    o5 = o4 + NA_WIDTH
    u_hy, u_fn, u_q, u_k, u_v, u_g = jnp.split(u, [o1, o2, o3, o4, o5], axis=-1)
    y_hy = hyena_branch(u_hy, p["hy_conv_w"], p["hy_conv_b"], p["hy_ffn_w1"], p["hy_ffn_b1"],
                        p["hy_ffn_w2"], p["hy_ffn_b2"], p["hy_ffn_w3"], p["hy_ffn_b3"],
                        p["hy_sin_freq"], p["hy_decay"], p["hy_skip"])
    y_fn = fnet_branch(u_fn)
    q = u_q.reshape(B, L, NA_HEADS, NA_HEAD_DIM)
    k = u_k.reshape(B, L, NA_HEADS, NA_HEAD_DIM)
    v = u_v.reshape(B, L, NA_HEADS, NA_HEAD_DIM)
    if k_ctx is None:
        y_na = context_attention(q, k, v)
    else:
        y_na = neighbourhood_attention(q, k, v, k_ctx, v_ctx, p["na_rel_bias"])
    gates = jax.nn.sigmoid(u_g.reshape(B, L, N_BRANCHES, D_MODEL))
    merged = (gates[:, :, 0] * (y_hy @ p["w_hy_out"])
              + gates[:, :, 1] * (y_fn @ p["w_fn_out"])
              + gates[:, :, 2] * (y_na @ p["w_na_out"]))
    x = x + g1 * (merged @ p["w_out"])
    h = rms_norm(x, p["norm_ffn"]) * (1 + sc2) + sh2
    x = x + g2 * moe_ffn(h, p["router_w"], p["router_b"], p["moe_w1"], p["moe_b1"],
                         p["moe_w2"], p["moe_b2"])
    return x, k, v


def setup_inputs(seed: int = 0) -> dict:
    key = jax.random.key(seed)
    ks = jax.random.split(key, 36)

    def nrm(i, shape, scale):
        return scale * jax.random.normal(ks[i], shape, jnp.float32)

    decay_min = math.log(1.0 / HY_DECAY_TARGET) / HY_SLOW_DECAY_PCT
    decay_max = math.log(1.0 / HY_DECAY_TARGET) / HY_FAST_DECAY_PCT
    decay_base = jnp.linspace(decay_min, decay_max, HY_WIDTH, dtype=jnp.float32)[None, :]
    return {
        "x_prompt": nrm(0, (BATCH, SEQ, D_MODEL), 1.0),
        "x_sample": nrm(1, (DEC_BATCH, DEC_SEQ, D_MODEL), 1.0),
        "cache_k": nrm(2, (DEC_BATCH, DEPTH, PAST_LEN, NA_HEADS, NA_HEAD_DIM), 1.0),
        "cache_v": nrm(3, (DEC_BATCH, DEPTH, PAST_LEN, NA_HEADS, NA_HEAD_DIM), 1.0),
        "c": nrm(4, (DEC_BATCH, D_MODEL), 1.0),
        "c_ctx": nrm(5, (D_MODEL,), 1.0),
        "ada_w": nrm(6, (DEPTH, D_MODEL, N_MOD * D_MODEL), 0.5 * D_MODEL ** -0.5),
        "ada_b": nrm(7, (DEPTH, N_MOD * D_MODEL), 0.02),
        "norm_mix": 1.0 + nrm(8, (DEPTH, D_MODEL), 0.05),
        "norm_ffn": 1.0 + nrm(9, (DEPTH, D_MODEL), 0.05),
        "w_in": nrm(10, (DEPTH, D_MODEL, IN_WIDTH), D_MODEL ** -0.5),
        "hy_conv_w": nrm(11, (DEPTH, SHORT_CONV, 1, 3 * HY_WIDTH), SHORT_CONV ** -0.5),
        "hy_conv_b": nrm(12, (DEPTH, 3 * HY_WIDTH), 0.02),
        "hy_ffn_w1": nrm(13, (DEPTH, HY_EMB, HY_FILTER_FF), HY_EMB ** -0.5),
        "hy_ffn_b1": nrm(14, (DEPTH, HY_FILTER_FF), 0.02),
        "hy_ffn_w2": nrm(15, (DEPTH, HY_FILTER_FF, HY_FILTER_FF), HY_FILTER_FF ** -0.5),
        "hy_ffn_b2": nrm(16, (DEPTH, HY_FILTER_FF), 0.02),
        "hy_ffn_w3": nrm(17, (DEPTH, HY_FILTER_FF, 2 * HY_WIDTH), HY_FILTER_FF ** -0.5),
        "hy_ffn_b3": nrm(18, (DEPTH, 2 * HY_WIDTH), 0.02),
        "hy_sin_freq": 1.0 + nrm(19, (DEPTH, HY_FILTER_FF), 0.1),
        "hy_decay": decay_base * (1.0 + nrm(20, (DEPTH, HY_WIDTH), 0.05)),
        "hy_skip": nrm(21, (DEPTH, HY_WIDTH), 1.0),
        "w_hy_out": nrm(22, (DEPTH, HY_WIDTH, D_MODEL), HY_WIDTH ** -0.5),
        "w_fn_out": nrm(23, (DEPTH, FN_WIDTH, D_MODEL), FN_WIDTH ** -0.5),
        "w_na_out": nrm(24, (DEPTH, NA_WIDTH, D_MODEL), NA_WIDTH ** -0.5),
        "na_rel_bias": nrm(25, (DEPTH, NA_HEADS, 2 * NA_WIN_ROWS - 1, 2 * NA_WIN_COLS - 1), 0.02),
        "w_out": nrm(26, (DEPTH, D_MODEL, D_MODEL), D_MODEL ** -0.5),
        "router_w": nrm(27, (DEPTH, D_MODEL, N_EXPERTS), D_MODEL ** -0.5),
        "router_b": nrm(28, (DEPTH, N_EXPERTS), 0.01),
        "moe_w1": nrm(29, (DEPTH, N_EXPERTS, D_MODEL, 2 * EXPERT_FF), D_MODEL ** -0.5),
        "moe_b1": nrm(30, (DEPTH, N_EXPERTS, 2 * EXPERT_FF), 0.02),
        "moe_w2": nrm(31, (DEPTH, N_EXPERTS, EXPERT_FF, D_MODEL), EXPERT_FF ** -0.5),
        "moe_b2": nrm(32, (DEPTH, N_EXPERTS, D_MODEL), 0.02),
        "final_norm": 1.0 + nrm(33, (D_MODEL,), 0.05),
    }


def reference(x_prompt, x_sample, cache_k, cache_v, c, c_ctx, ada_w, ada_b, norm_mix, norm_ffn,
              w_in, hy_conv_w, hy_conv_b, hy_ffn_w1, hy_ffn_b1, hy_ffn_w2, hy_ffn_b2, hy_ffn_w3,
              hy_ffn_b3, hy_sin_freq, hy_decay, hy_skip, w_hy_out, w_fn_out, w_na_out,
              na_rel_bias, w_out, router_w, router_b, moe_w1, moe_b1, moe_w2, moe_b2, final_norm):
    xp = x_prompt
    xs = x_sample
    new_k = []
    new_v = []
    for l in range(DEPTH):
        p = {
            "ada_w": ada_w[l], "ada_b": ada_b[l], "norm_mix": norm_mix[l], "norm_ffn": norm_ffn[l],
            "w_in": w_in[l], "hy_conv_w": hy_conv_w[l], "hy_conv_b": hy_conv_b[l],
            "hy_ffn_w1": hy_ffn_w1[l], "hy_ffn_b1": hy_ffn_b1[l], "hy_ffn_w2": hy_ffn_w2[l],
            "hy_ffn_b2": hy_ffn_b2[l], "hy_ffn_w3": hy_ffn_w3[l], "hy_ffn_b3": hy_ffn_b3[l],
            "hy_sin_freq": hy_sin_freq[l], "hy_decay": hy_decay[l], "hy_skip": hy_skip[l],
            "w_hy_out": w_hy_out[l], "w_fn_out": w_fn_out[l], "w_na_out": w_na_out[l],
            "na_rel_bias": na_rel_bias[l], "w_out": w_out[l], "router_w": router_w[l],
            "router_b": router_b[l], "moe_w1": moe_w1[l], "moe_b1": moe_b1[l],
            "moe_w2": moe_w2[l], "moe_b2": moe_b2[l],
        }
        xp, k_l, v_l = trunk_layer(xp, c_ctx[None, :], p)
        new_k.append(k_l)
        new_v.append(v_l)
        xs, _, _ = trunk_layer(xs, c, p, cache_k[:, l], cache_v[:, l])
    y_prompt = rms_norm(xp, final_norm)
    y_sample = rms_norm(xs, final_norm)
    new_cache_k = jnp.stack(new_k, axis=1)
    new_cache_v = jnp.stack(new_v, axis=1)
    return (y_prompt, y_sample, new_cache_k, new_cache_v)
```

```python
import functools
import math

import numpy as np
import jax
import jax.numpy as jnp
from jax import lax
from jax.experimental import pallas as pl
from jax.experimental.pallas import tpu as pltpu

F32 = jnp.float32
BF16 = jnp.bfloat16
I32 = jnp.int32

D = 1024
DEPTH = 4
N_MOD = 6
RMS_EPS = 1e-6
HY_W = 384
HY_EMB = 33
HY_EMB_PAD = 128
HY_FF = 64
FN_W = 256
FN_GROUPS = 4
FN_GROUP_DIM = 64
NA_H = 6
NA_D = 64
NA_W = NA_H * NA_D
GRID_W = 64
WIN_R = 8
WIN_C = 16
ATTN_SCALE = NA_D ** -0.5
BR_W = 3 * HY_W + FN_W + 3 * NA_W
N_EXP = 32
TOP_K = 4
FF = 1024
SWIGLU_LIMIT = 7.0
SWIGLU_ALPHA = 1.702
MASK_NEG = -1e30

ROW_TILE = 256
MOE_ROWS = 512
DFT_TILE = 512
VMEM_LIMIT = 56 * 1024 * 1024


def _cp(sem, vmem=VMEM_LIMIT):
    return pltpu.CompilerParams(dimension_semantics=sem, vmem_limit_bytes=vmem)


def _dot(a, b):
    return jnp.dot(a, b, preferred_element_type=F32)


def _dot_hi(a, b):
    return jnp.dot(a, b, preferred_element_type=F32, precision=lax.Precision.HIGHEST)


def _dot_nt(a, b):
    return lax.dot_general(a, b, (((1,), (1,)), ((), ())), preferred_element_type=F32)


def _mod_row(mod_ref, cnd, k):
    return mod_ref[pl.ds(cnd, 1), pl.ds(k * D, D)]


def _rms(x, g):
    return x * lax.rsqrt(jnp.mean(x * x, axis=-1, keepdims=True) + RMS_EPS) * g


def _mod_kernel(cond_ref, w_ref, b_ref, o_ref):
    c = cond_ref[...]
    s = c * jax.nn.sigmoid(c)
    o_ref[...] = _dot(s.astype(BF16), w_ref[...].astype(BF16)) + b_ref[...]


def _mod_call(cond8, ada_w, ada_b):
    tn = 1536
    return pl.pallas_call(
        _mod_kernel,
        grid=(DEPTH, N_MOD * D // tn),
        in_specs=[pl.BlockSpec((8, D), lambda l, j: (0, 0)),
                  pl.BlockSpec((None, D, tn), lambda l, j: (l, 0, j)),
                  pl.BlockSpec((None, 1, tn), lambda l, j: (l, 0, j))],
        out_specs=pl.BlockSpec((None, 8, tn), lambda l, j: (l, 0, j)),
        out_shape=jax.ShapeDtypeStruct((DEPTH, 8, N_MOD * D), F32),
        compiler_params=_cp(("parallel", "parallel")),
        name="adaln_mod",
    )(cond8, ada_w, ada_b.reshape(DEPTH, 1, N_MOD * D))


def _inproj_kernel(x_ref, mod_ref, g_ref, w_ref, hy_ref, fn_ref, q_ref, k_ref, v_ref, *,
                   cond_base, rows_per_cond, tm):
    cnd = cond_base + (pl.program_id(0) * tm) // rows_per_cond
    h = _rms(x_ref[...], g_ref[...]) * (1.0 + _mod_row(mod_ref, cnd, 1)) + _mod_row(mod_ref, cnd, 0)
    u = _dot(h.astype(BF16), w_ref[...])
    o1 = 3 * HY_W
    o2 = o1 + FN_W
    o3 = o2 + NA_W
    o4 = o3 + NA_W
    hy_ref[...] = u[:, :o1].astype(BF16)
    fn_ref[...] = u[:, o1:o2].astype(BF16)
    q_ref[...] = u[:, o2:o3].astype(BF16)
    k_ref[...] = u[:, o3:o4]
    v_ref[...] = u[:, o4:]


def _inproj_call(l, x, mod, norm_mix, w_in_b, cond_base, rows_per_cond):
    n = x.shape[0]
    tm = ROW_TILE
    kern = functools.partial(_inproj_kernel, cond_base=cond_base, rows_per_cond=rows_per_cond, tm=tm)
    widths = (3 * HY_W, FN_W, NA_W, NA_W, NA_W)
    dts = (BF16, BF16, BF16, F32, F32)
    return pl.pallas_call(
        kern,
        grid=(n // tm,),
        in_specs=[pl.BlockSpec((tm, D), lambda i: (i, 0)),
                  pl.BlockSpec((None, 8, N_MOD * D), lambda i: (l, 0, 0)),
                  pl.BlockSpec((None, 1, D), lambda i: (l, 0, 0)),
                  pl.BlockSpec((None, D, BR_W), lambda i: (l, 0, 0))],
        out_specs=[pl.BlockSpec((tm, w), lambda i: (i, 0)) for w in widths],
        out_shape=[jax.ShapeDtypeStruct((n, w), dt) for w, dt in zip(widths, dts)],
        compiler_params=_cp(("parallel",)),
        name="in_proj",
    )(x, mod, norm_mix, w_in_b)


def _cs_tables(L, N):
    k = np.arange(L, dtype=np.int64)[:, None]
    na = np.arange(0, L, 64, dtype=np.int64)[None, :]
    nb = np.arange(64, dtype=np.int64)[None, :]
    ang_a = 2.0 * np.pi * ((k * na) % N).astype(np.float64) / N
    ang_b = 2.0 * np.pi * ((k * nb) % N).astype(np.float64) / N
    ca = jnp.asarray(np.cos(ang_a), F32)[:, :, None]
    sa = jnp.asarray(np.sin(ang_a), F32)[:, :, None]
    cb = jnp.asarray(np.cos(ang_b), F32)[:, None, :]
    sb = jnp.asarray(np.sin(ang_b), F32)[:, None, :]
    cos = (ca * cb - sa * sb).reshape(L, L)
    sin = (sa * cb + ca * sb).reshape(L, L)
    return cos, sin


def _hyena_tables(L):
    cos, sin = _cs_tables(L, 2 * L)
    alt = (1 - 2 * (jnp.arange(L) % 2)).astype(F32)
    row0 = (jnp.arange(L) == 0)
    s_fwd = jnp.where(row0[:, None], alt[None, :], sin)
    s_inv = jnp.where(row0[None, :], alt[:, None], sin)
    return cos.astype(BF16), s_fwd.astype(BF16), s_inv.astype(BF16)


def _fnet_tables(L):
    cos, sin = _cs_tables(L, L)
    g = FN_GROUP_DIM
    kk = np.arange(g)[:, None] * np.arange(g)[None, :]
    ang = 2.0 * np.pi * (kk % g) / g
    scale = 1.0 / math.sqrt(L * g)
    cb = np.kron(np.eye(FN_GROUPS), np.cos(ang)) * scale
    sb = np.kron(np.eye(FN_GROUPS), np.sin(ang)) * scale
    csb = jnp.asarray(np.concatenate([cb, sb], axis=1), F32).astype(BF16)
    return cos.astype(BF16), sin.astype(BF16), csb


def _hy_filt_a_kernel(z_ref, w1_ref, b1_ref, w2_ref, b2_ref, w3a_ref, w3b_ref, b3a_ref, b3b_ref,
                      fr_ref, dec_ref, fp_ref, fm_ref, last_ref, nyq_ref, pad_ref, *, L):
    z = z_ref[...]
    fr = fr_ref[...]
    a = jnp.sin(fr * (_dot_hi(z, w1_ref[...]) + b1_ref[...]))
    a = jnp.sin(fr * (_dot_hi(a, w2_ref[...]) + b2_ref[...]))
    window = jnp.exp(-z[:, 0:1] * jnp.abs(dec_ref[...]))
    f0 = (_dot_hi(a, w3a_ref[...]) + b3a_ref[...]) * window
    f1 = (_dot_hi(a, w3b_ref[...]) + b3b_ref[...]) * window
    norm = (jnp.sum(jnp.abs(f0), axis=0, keepdims=True)
            + jnp.sum(jnp.abs(f1), axis=0, keepdims=True) + 1e-6)
    f0 = f0 / norm
    f1 = f1 / norm
    pad_ref[pl.ds(0, 8), :] = jnp.zeros((8, 128), F32)
    pad_ref[pl.ds(8, L), :] = f1
    f1s = pad_ref[pl.ds(7, L), :]
    last = pad_ref[pl.ds(L + 7, 1), :]
    fp = f0 + f1s
    fp_ref[...] = fp.astype(BF16)
    fm_ref[...] = (f0 - f1s).astype(BF16)
    last_ref[...] = last
    row = lax.broadcasted_iota(I32, (L, 128), 0)
    alt = (1 - 2 * (row & 1)).astype(F32)
    nyq_ref[...] = jnp.sum(alt * fp, axis=0, keepdims=True) + last


def _hy_filt_b_kernel(c_ref, s_ref, fp_ref, fm_ref, last_ref, nyq_ref, hc_ref, hs_ref, *, tk):
    kt = pl.program_id(0)
    row = kt * tk + lax.broadcasted_iota(I32, (tk, HY_W), 0)
    alt = (1 - 2 * (row & 1)).astype(F32)
    hc_ref[...] = _dot(c_ref[...], fp_ref[...]) + alt * last_ref[...]
    hs = _dot(s_ref[...], fm_ref[...])
    hs_ref[...] = jnp.where(row == 0, nyq_ref[...], hs)


def _hyena_filters(L, tabs, p):
    cos, s_fwd, _ = tabs
    t = np.linspace(0.0, 1.0, L, dtype=np.float32)[:, None]
    bands = (HY_EMB - 1) // 2
    omega = (2.0 * math.pi * np.arange(L, dtype=np.float32)[:, None] / L).astype(np.float32)
    f = np.linspace(1e-4, bands - 1, bands, dtype=np.float32)[None, :]
    z = np.concatenate([t, np.cos(f * omega), -np.sin(f * omega)], axis=-1).astype(np.float32)
    z = jnp.asarray(np.pad(z, ((0, 0), (0, HY_EMB_PAD - HY_EMB))))
    w1 = jnp.pad(p["hy_ffn_w1"], ((0, 0), (0, HY_EMB_PAD - HY_EMB), (0, 0)))
    r3 = lambda a: a.reshape(DEPTH, 1, a.shape[-1])
    nch = HY_W // 128
    vec = lambda: pl.BlockSpec((None, 1, HY_FF), lambda l, c: (l, 0, 0))
    fp, fm, last, nyq = pl.pallas_call(
        functools.partial(_hy_filt_a_kernel, L=L),
        grid=(DEPTH, nch),
        in_specs=[pl.BlockSpec((L, HY_EMB_PAD), lambda l, c: (0, 0)),
                  pl.BlockSpec((None, HY_EMB_PAD, HY_FF), lambda l, c: (l, 0, 0)), vec(),
                  pl.BlockSpec((None, HY_FF, HY_FF), lambda l, c: (l, 0, 0)), vec(),
                  pl.BlockSpec((None, HY_FF, 128), lambda l, c: (l, 0, c)),
                  pl.BlockSpec((None, HY_FF, 128), lambda l, c: (l, 0, nch + c)),
                  pl.BlockSpec((None, 1, 128), lambda l, c: (l, 0, c)),
                  pl.BlockSpec((None, 1, 128), lambda l, c: (l, 0, nch + c)),
                  vec(),
                  pl.BlockSpec((None, 1, 128), lambda l, c: (l, 0, c))],
        out_specs=[pl.BlockSpec((None, L, 128), lambda l, c: (l, 0, c)),
                   pl.BlockSpec((None, L, 128), lambda l, c: (l, 0, c)),
                   pl.BlockSpec((None, 1, 128), lambda l, c: (l, 0, c)),
                   pl.BlockSpec((None, 1, 128), lambda l, c: (l, 0, c))],
        out_shape=[jax.ShapeDtypeStruct((DEPTH, L, HY_W), BF16),
                   jax.ShapeDtypeStruct((DEPTH, L, HY_W), BF16),
                   jax.ShapeDtypeStruct((DEPTH, 1, HY_W), F32),
                   jax.ShapeDtypeStruct((DEPTH, 1, HY_W), F32)],
        scratch_shapes=[pltpu.VMEM((L + 8, 128), F32)],
        compiler_params=_cp(("parallel", "parallel")),
        name="hyena_filter_taps",
    )(z, w1, r3(p["hy_ffn_b1"]), p["hy_ffn_w2"], r3(p["hy_ffn_b2"]), p["hy_ffn_w3"], p["hy_ffn_w3"],
      r3(p["hy_ffn_b3"]), r3(p["hy_ffn_b3"]), r3(p["hy_sin_freq"]), r3(p["hy_decay"]))
    tk = min(DFT_TILE, L)
    full = lambda: pl.BlockSpec((None, L, HY_W), lambda kt, l: (l, 0, 0))
    one = lambda: pl.BlockSpec((None, 1, HY_W), lambda kt, l: (l, 0, 0))
    hc, hs = pl.pallas_call(
        functools.partial(_hy_filt_b_kernel, tk=tk),
        grid=(L // tk, DEPTH),
        in_specs=[pl.BlockSpec((tk, L), lambda kt, l: (kt, 0)),
                  pl.BlockSpec((tk, L), lambda kt, l: (kt, 0)),
                  full(), full(), one(), one()],
        out_specs=[pl.BlockSpec((None, tk, HY_W), lambda kt, l: (l, kt, 0)),
                   pl.BlockSpec((None, tk, HY_W), lambda kt, l: (l, kt, 0))],
        out_shape=[jax.ShapeDtypeStruct((DEPTH, L, HY_W), F32)] * 2,
        compiler_params=_cp(("parallel", "parallel")),
        name="hyena_filter_dft",
    )(cos, s_fwd, fp, fm, last, nyq)
    return hc, hs


def _hy_pre_kernel(u0_ref, u1_ref, u2_ref, w0_ref, w1_ref, w2_ref, b0_ref, b1_ref, b2_ref,
                   zz_ref, x0_ref, pad_ref, *, L):
    zeros8 = jnp.zeros((8, 128), F32)
    pad_ref[pl.ds(0, 8), :] = zeros8
    pad_ref[pl.ds(L + 8, 8), :] = zeros8

    def conv(u_ref, w_ref, b_ref):
        pad_ref[pl.ds(8, L), :] = u_ref[...].astype(F32)
        w = w_ref[...]
        return (w[0:1] * pad_ref[pl.ds(7, L), :] + w[1:2] * pad_ref[pl.ds(8, L), :]
                + w[2:3] * pad_ref[pl.ds(9, L), :] + b_ref[...])

    x0_ref[...] = conv(u0_ref, w0_ref, b0_ref).astype(BF16)
    x1 = conv(u1_ref, w1_ref, b1_ref)
    v = conv(u2_ref, w2_ref, b2_ref)
    zz_ref[...] = (x1 * v).astype(BF16)


def _hy_fwd_kernel(c_ref, s_ref, zz_ref, hc_ref, hs_ref, pr_ref, pi_ref, *, tk, L):
    kt = pl.program_id(0)
    zz = zz_ref[...]
    zc = _dot(c_ref[...], zz)
    zs = _dot(s_ref[...], zz)
    hc = hc_ref[...]
    hs = hs_ref[...]
    row = kt * tk + lax.broadcasted_iota(I32, (tk, HY_W), 0)
    first = row == 0
    wgt = jnp.where(first, 1.0 / (2 * L), 2.0 / (2 * L)).astype(F32)
    ss = zs * hs
    pr_ref[...] = ((zc * hc - jnp.where(first, 0.0, ss)) * wgt).astype(BF16)
    pi_ref[...] = (jnp.where(first, ss, zc * hs + zs * hc) * wgt).astype(BF16)


def _hy_inv_kernel(c_ref, st_ref, pr_ref, pi_ref, zz_ref, x0_ref, skip_ref, o_ref):
    y = _dot(c_ref[...], pr_ref[...]) + _dot(st_ref[...], pi_ref[...])
    zz = zz_ref[...].astype(F32)
    o_ref[...] = (x0_ref[...].astype(F32) * (y + zz * skip_ref[...])).astype(BF16)


def _hyena_call(l, u_hy, B, L, tabs, hc, hs, p):
    cos, s_fwd, s_inv = tabs
    n = B * L
    nch = HY_W // 128
    cw = p["hy_conv_w"].reshape(DEPTH, 3, 3 * HY_W)
    cb = p["hy_conv_b"].reshape(DEPTH, 1, 3 * HY_W)
    ub = lambda part: pl.BlockSpec((L, 128), lambda b, c: (b, part * nch + c))
    wb = lambda part: pl.BlockSpec((None, 3, 128), lambda b, c: (l, 0, part * nch + c))
    bb = lambda part: pl.BlockSpec((None, 1, 128), lambda b, c: (l, 0, part * nch + c))
    zz, x0 = pl.pallas_call(
        functools.partial(_hy_pre_kernel, L=L),
        grid=(B, nch),
        in_specs=[ub(0), ub(1), ub(2), wb(0), wb(1), wb(2), bb(0), bb(1), bb(2)],
        out_specs=[pl.BlockSpec((L, 128), lambda b, c: (b, c))] * 2,
        out_shape=[jax.ShapeDtypeStruct((n, HY_W), BF16)] * 2,
        scratch_shapes=[pltpu.VMEM((L + 16, 128), F32)],
        compiler_params=_cp(("parallel", "parallel")),
        name="hyena_short_conv",
    )(u_hy, u_hy, u_hy, cw, cw, cw, cb, cb, cb)

    tk = min(DFT_TILE, L)
    nk = L // tk
    pr, pi = pl.pallas_call(
        functools.partial(_hy_fwd_kernel, tk=tk, L=L),
        grid=(nk, B),
        in_specs=[pl.BlockSpec((tk, L), lambda kt, b: (kt, 0)),
                  pl.BlockSpec((tk, L), lambda kt, b: (kt, 0)),
                  pl.BlockSpec((L, HY_W), lambda kt, b: (b, 0)),
                  pl.BlockSpec((None, tk, HY_W), lambda kt, b: (l, kt, 0)),
                  pl.BlockSpec((None, tk, HY_W), lambda kt, b: (l, kt, 0))],
        out_specs=[pl.BlockSpec((tk, HY_W), lambda kt, b: (b * nk + kt, 0))] * 2,
        out_shape=[jax.ShapeDtypeStruct((n, HY_W), BF16)] * 2,
        compiler_params=_cp(("parallel", "parallel")),
        name="hyena_dft_fwd",
    )(cos, s_fwd, zz, hc, hs)

    y = pl.pallas_call(
        _hy_inv_kernel,
        grid=(nk, B),
        in_specs=[pl.BlockSpec((tk, L), lambda qt, b: (qt, 0)),
                  pl.BlockSpec((tk, L), lambda qt, b: (qt, 0)),
                  pl.BlockSpec((L, HY_W), lambda qt, b: (b, 0)),
                  pl.BlockSpec((L, HY_W), lambda qt, b: (b, 0)),
                  pl.BlockSpec((tk, HY_W), lambda qt, b: (b * nk + qt, 0)),
                  pl.BlockSpec((tk, HY_W), lambda qt, b: (b * nk + qt, 0)),
                  pl.BlockSpec((None, 1, HY_W), lambda qt, b: (l, 0, 0))],
        out_specs=pl.BlockSpec((tk, HY_W), lambda qt, b: (b * nk + qt, 0)),
        out_shape=jax.ShapeDtypeStruct((n, HY_W), BF16),
        compiler_params=_cp(("parallel", "parallel")),
        name="hyena_dft_inv",
    )(cos, s_inv, pr, pi, zz, x0, p["hy_skip"].reshape(DEPTH, 1, HY_W))
    return y


def _fnet_kernel(u_ref, csb_ref, c_ref, s_ref, o_ref, uc_ref, us_ref):
    @pl.when(pl.program_id(1) == 0)
    def _():
        t = _dot(u_ref[...], csb_ref[...])
        uc_ref[...] = t[:, :FN_W].astype(BF16)
        us_ref[...] = t[:, FN_W:].astype(BF16)

    o_ref[...] = (_dot(c_ref[...], uc_ref[...]) - _dot(s_ref[...], us_ref[...])).astype(BF16)


def _fnet_call(u_fn, B, L, tabs):
    cos, sin, csb = tabs
    tq = min(DFT_TILE, L)
    nq = L // tq
    return pl.pallas_call(
        _fnet_kernel,
        grid=(B, nq),
        in_specs=[pl.BlockSpec((L, FN_W), lambda b, qt: (b, 0)),
                  pl.BlockSpec((FN_W, 2 * FN_W), lambda b, qt: (0, 0)),
                  pl.BlockSpec((tq, L), lambda b, qt: (qt, 0)),
                  pl.BlockSpec((tq, L), lambda b, qt: (qt, 0))],
        out_specs=pl.BlockSpec((tq, FN_W), lambda b, qt: (b * nq + qt, 0)),
        out_shape=jax.ShapeDtypeStruct((B * L, FN_W), BF16),
        scratch_shapes=[pltpu.VMEM((L, FN_W), BF16), pltpu.VMEM((L, FN_W), BF16)],
        compiler_params=_cp(("parallel", "arbitrary")),
        name="fnet_dft2",
    )(u_fn, csb, cos, sin)


def _ctx_attn_kernel(q_ref, k_ref, v_ref, o_ref):
    for h in range(NA_H):
        sl = slice(h * NA_D, (h + 1) * NA_D)
        q = q_ref[:, sl]
        k = k_ref[:, sl].astype(BF16)
        v = v_ref[:, sl].astype(BF16)
        s = _dot_nt(q, k) * ATTN_SCALE
        m = jnp.max(s, axis=-1, keepdims=True)
        e = jnp.exp(s - m)
        den = jnp.sum(e, axis=-1, keepdims=True)
        o_ref[:, sl] = (_dot(e.astype(BF16), v) / den).astype(BF16)


def _ctx_attn_call(q, k, v, B, L):
    blk = lambda: pl.BlockSpec((L, NA_W), lambda b: (b, 0))
    return pl.pallas_call(
        _ctx_attn_kernel,
        grid=(B,),
        in_specs=[blk(), blk(), blk()],
        out_specs=blk(),
        out_shape=jax.ShapeDtypeStruct((B * L, NA_W), BF16),
        compiler_params=_cp(("parallel",)),
        name="context_attention",
    )(q, k, v)


def _nbr_attn_kernel(q_ref, k_ref, v_ref, kc_ref, vc_ref, bias_ref, o_ref, *, rows, wr):
    r = pl.program_id(1)
    rs = jnp.clip(r - wr // 2, 0, rows - wr)
    off = rs - r + WIN_R - 1
    start = pl.multiple_of(rs * GRID_W, GRID_W)
    band = wr * GRID_W
    for h in range(NA_H):
        sl = slice(h * NA_D, (h + 1) * NA_D)
        q = q_ref[:, sl]
        kb = k_ref[pl.ds(start, band), sl].astype(BF16)
        vb = v_ref[pl.ds(start, band), sl].astype(BF16)
        kc = kc_ref[:, sl].astype(BF16)
        vc = vc_ref[:, sl].astype(BF16)
        s_loc = _dot_nt(q, kb) * ATTN_SCALE + bias_ref[h, off]
        s_ctx = _dot_nt(q, kc) * ATTN_SCALE
        m = jnp.maximum(jnp.max(s_loc, axis=-1, keepdims=True), jnp.max(s_ctx, axis=-1, keepdims=True))
        e_loc = jnp.exp(s_loc - m)
        e_ctx = jnp.exp(s_ctx - m)
        den = jnp.sum(e_loc, axis=-1, keepdims=True) + jnp.sum(e_ctx, axis=-1, keepdims=True)
        o = _dot(e_loc.astype(BF16), vb) + _dot(e_ctx.astype(BF16), vc)
        o_ref[:, sl] = (o / den).astype(BF16)


def _nbr_bias_table(rel_bias, wr):
    cols = np.arange(GRID_W)
    cs = np.clip(cols - WIN_C // 2, 0, GRID_W - WIN_C)
    mask = (cols[None, :] >= cs[:, None]) & (cols[None, :] < cs[:, None] + WIN_C)
    dc = np.clip(cols[None, :] - cols[:, None], -(WIN_C - 1), WIN_C - 1) + WIN_C - 1
    dr = np.arange(WIN_R)[:, None] + np.arange(wr)[None, :]
    dr = np.minimum(dr, 2 * WIN_R - 2)
    t = rel_bias[:, :, dr[:, None, :, None], dc[None, :, None, :]]
    t = jnp.where(jnp.asarray(mask)[None, None, None, :, None, :], t.astype(F32), MASK_NEG)
    return t.reshape(DEPTH, NA_H, WIN_R, GRID_W, wr * GRID_W)


def _nbr_attn_call(l, q, k, v, ck, cv, bias, B, L):
    rows = L // GRID_W
    wr = min(WIN_R, rows)
    P = ck.shape[2]
    return pl.pallas_call(
        functools.partial(_nbr_attn_kernel, rows=rows, wr=wr),
        grid=(B, rows),
        in_specs=[pl.BlockSpec((GRID_W, NA_W), lambda b, r: (b * rows + r, 0)),
                  pl.BlockSpec((L, NA_W), lambda b, r: (b, 0)),
                  pl.BlockSpec((L, NA_W), lambda b, r: (b, 0)),
                  pl.BlockSpec((None, None, P, NA_W), lambda b, r: (b, l, 0, 0)),
                  pl.BlockSpec((None, None, P, NA_W), lambda b, r: (b, l, 0, 0)),
                  pl.BlockSpec((None, NA_H, WIN_R, GRID_W, wr * GRID_W), lambda b, r: (l, 0, 0, 0, 0))],
        out_specs=pl.BlockSpec((GRID_W, NA_W), lambda b, r: (b * rows + r, 0)),
        out_shape=jax.ShapeDtypeStruct((B * L, NA_W), BF16),
        compiler_params=_cp(("parallel", "parallel")),
        name="neighbourhood_attention",
    )(q, k, v, ck, cv, bias)


def _merge_kernel(x_ref, yhy_ref, yfn_ref, yna_ref, mod_ref, gmix_ref, gffn_ref, wg_ref, why_ref,
                  wfn_ref, wna_ref, wout_ref, rw_ref, rb_ref, cnt0_ref,
                  xn_ref, h2_ref, idx_ref, gate_ref, rank_ref, cnt_ref, run_ref, *,
                  cond_base, rows_per_cond, tm):
    i = pl.program_id(0)
    cnd = cond_base + (i * tm) // rows_per_cond

    @pl.when(i == 0)
    def _():
        run_ref[...] = cnt0_ref[...]

    x = x_ref[...]
    h = _rms(x, gmix_ref[...]) * (1.0 + _mod_row(mod_ref, cnd, 1)) + _mod_row(mod_ref, cnd, 0)
    ug = _dot(h.astype(BF16), wg_ref[...])
    gates = jax.nn.sigmoid(ug)
    merged = (gates[:, :D] * _dot(yhy_ref[...], why_ref[...])
              + gates[:, D:2 * D] * _dot(yfn_ref[...], wfn_ref[...])
              + gates[:, 2 * D:] * _dot(yna_ref[...], wna_ref[...]))
    xn = x + _mod_row(mod_ref, cnd, 2) * _dot(merged.astype(BF16), wout_ref[...])
    xn_ref[...] = xn
    h2 = _rms(xn, gffn_ref[...]) * (1.0 + _mod_row(mod_ref, cnd, 4)) + _mod_row(mod_ref, cnd, 3)
    h2_ref[...] = h2

    logits = _dot_hi(h2, rw_ref[...]) + rb_ref[...]
    lane = lax.broadcasted_iota(I32, (tm, N_EXP), 1)
    lane4 = lax.broadcasted_iota(I32, (tm, TOP_K), 1)
    rem = logits
    vals, idxs = [], []
    for _ in range(TOP_K):
        m = jnp.max(rem, axis=-1, keepdims=True)
        ix = jnp.min(jnp.where(rem == m, lane, N_EXP), axis=-1, keepdims=True)
        vals.append(m)
        idxs.append(ix)
        rem = jnp.where(lane == ix, -jnp.inf, rem)
    es = [jnp.exp(v - vals[0]) for v in vals]
    den = es[0] + es[1] + es[2] + es[3]

    onehot = jnp.zeros((tm, N_EXP), F32)
    for ix in idxs:
        onehot = onehot + (lane == ix).astype(F32)
    r_i = lax.broadcasted_iota(I32, (tm, tm), 0)
    c_i = lax.broadcasted_iota(I32, (tm, tm), 1)
    lower = (c_i < r_i).astype(BF16)
    before = _dot(lower, onehot.astype(BF16)) + run_ref[...]

    idx_o = jnp.zeros((tm, TOP_K), I32)
    gate_o = jnp.zeros((tm, TOP_K), F32)
    rank_o = jnp.zeros((tm, TOP_K), F32)
    for kk in range(TOP_K):
        rk = jnp.sum(jnp.where(lane == idxs[kk], before, 0.0), axis=-1, keepdims=True)
        idx_o = jnp.where(lane4 == kk, idxs[kk], idx_o)
        gate_o = jnp.where(lane4 == kk, es[kk] / den, gate_o)
        rank_o = jnp.where(lane4 == kk, rk, rank_o)
    idx_ref[...] = idx_o
    gate_ref[...] = gate_o
    rank_ref[...] = rank_o.astype(I32)
    run_ref[...] = run_ref[...] + jnp.sum(onehot, axis=0, keepdims=True)
    cnt_ref[...] = run_ref[...]


def _merge_call(l, x, y_hy, y_fn, y_na, mod, wts, cnt0, cond_base, rows_per_cond):
    n = x.shape[0]
    tm = ROW_TILE
    kern = functools.partial(_merge_kernel, cond_base=cond_base, rows_per_cond=rows_per_cond, tm=tm)
    row = lambda w: pl.BlockSpec((tm, w), lambda i: (i, 0))
    lw = lambda a, b: pl.BlockSpec((None, a, b), lambda i: (l, 0, 0))
    return pl.pallas_call(
        kern,
        grid=(n // tm,),
        in_specs=[row(D), row(HY_W), row(FN_W), row(NA_W),
                  lw(8, N_MOD * D), lw(1, D), lw(1, D),
                  lw(D, 3 * D), lw(HY_W, D), lw(FN_W, D), lw(NA_W, D), lw(D, D),
                  lw(D, N_EXP), lw(1, N_EXP),
                  pl.BlockSpec((1, N_EXP), lambda i: (0, 0))],
        out_specs=[row(D), row(D), row(TOP_K), row(TOP_K), row(TOP_K),
                   pl.BlockSpec((1, N_EXP), lambda i: (0, 0))],
        out_shape=[jax.ShapeDtypeStruct((n, D), F32), jax.ShapeDtypeStruct((n, D), F32),
                   jax.ShapeDtypeStruct((n, TOP_K), I32), jax.ShapeDtypeStruct((n, TOP_K), F32),
                   jax.ShapeDtypeStruct((n, TOP_K), I32), jax.ShapeDtypeStruct((1, N_EXP), F32)],
        scratch_shapes=[pltpu.VMEM((1, N_EXP), F32)],
        compiler_params=_cp(("arbitrary",)),
        name="merge_router",
    )(x, y_hy, y_fn, y_na, mod, wts["norm_mix"], wts["norm_ffn"], wts["w_g"], wts["w_hy_out"],
      wts["w_fn_out"], wts["w_na_out"], wts["w_out"], wts["router_w"], wts["router_b"], cnt0)


def _dispatch_kernel(slot_ref, h_ref, xs_in_ref, xs_ref, sem, *, tm):
    del xs_in_ref

    def row_copy(t, s):
        return pltpu.make_async_copy(h_ref.at[pl.ds(t, 1)], xs_ref.at[pl.ds(s, 1)], sem)

    def issue(t, carry):
        for kk in range(TOP_K):
            row_copy(t, slot_ref[0, 0, t * TOP_K + kk]).start()
        return carry

    def drain(t, carry):
        for kk in range(TOP_K):
            row_copy(t, slot_ref[0, 0, t * TOP_K + kk]).wait()
        return carry

    lax.fori_loop(0, tm, issue, 0)
    lax.fori_loop(0, tm, drain, 0)


def _dispatch_call(slot, h2, xs):
    n = h2.shape[0]
    tm = ROW_TILE
    slot3 = slot.reshape(n // tm, 1, tm * TOP_K)
    return pl.pallas_call(
        functools.partial(_dispatch_kernel, tm=tm),
        grid=(n // tm,),
        in_specs=[pl.BlockSpec((1, 1, tm * TOP_K), lambda i: (i, 0, 0), memory_space=pltpu.SMEM),
                  pl.BlockSpec((tm, D), lambda i: (i, 0)),
                  pl.BlockSpec(memory_space=pl.ANY)],
        out_specs=pl.BlockSpec(memory_space=pl.ANY),
        out_shape=jax.ShapeDtypeStruct(xs.shape, xs.dtype),
        scratch_shapes=[pltpu.SemaphoreType.DMA(())],
        input_output_aliases={2: 0},
        compiler_params=_cp(("arbitrary",)),
        name="moe_dispatch",
    )(slot3, h2, xs)


def _moe_kernel(be_ref, nu_ref, x_ref, w1g_ref, w1l_ref, b1g_ref, b1l_ref, w2_ref, b2_ref, y_ref):
    del be_ref
    j = pl.program_id(0)

    @pl.when(j < nu_ref[0])
    def _():
        x = x_ref[...].astype(BF16)
        g = jnp.minimum(_dot(x, w1g_ref[...]) + b1g_ref[...], SWIGLU_LIMIT)
        lin = jnp.clip(_dot(x, w1l_ref[...]) + b1l_ref[...], -SWIGLU_LIMIT, SWIGLU_LIMIT)
        act = g * jax.nn.sigmoid(SWIGLU_ALPHA * g) * (lin + 1.0)
        y_ref[...] = _dot(act.astype(BF16), w2_ref[...]) + b2_ref[...]

    @pl.when(j >= nu_ref[0])
    def _():
        y_ref[...] = jnp.zeros_like(y_ref)


def _moe_call(l, xs, block_expert, n_used, wts):
    n_slots = xs.shape[0]
    R = MOE_ROWS
    wspec = lambda: pl.BlockSpec((None, None, D, FF), lambda j, be, nu: (l, be[j], 0, 0))
    bspec = lambda: pl.BlockSpec((None, None, 1, FF), lambda j, be, nu: (l, be[j], 0, 0))
    grid_spec = pltpu.PrefetchScalarGridSpec(
        num_scalar_prefetch=2,
        grid=(n_slots // R,),
        in_specs=[pl.BlockSpec((R, D), lambda j, be, nu: (j, 0)),
                  wspec(), wspec(), bspec(), bspec(), wspec(), bspec()],
        out_specs=pl.BlockSpec((R, D), lambda j, be, nu: (j, 0)),
    )
    return pl.pallas_call(
        _moe_kernel,
        grid_spec=grid_spec,
        out_shape=jax.ShapeDtypeStruct((n_slots, D), F32),
        compiler_params=_cp(("arbitrary",)),
        name="moe_experts",
    )(block_expert, n_used, xs, wts["w1g"], wts["w1l"], wts["b1g"], wts["b1l"], wts["w2"], wts["b2"])


def _combine_kernel(slot_ref, xn_ref, gate_ref, mod_ref, fin_ref, ys_ref, o_ref, ybuf, sem, *,
                    cond_base, rows_per_cond, tm, final):
    cnd = cond_base + (pl.program_id(0) * tm) // rows_per_cond

    def row_copy(t, kk, s):
        return pltpu.make_async_copy(ys_ref.at[pl.ds(s, 1)], ybuf.at[kk, pl.ds(t, 1)], sem)

    def issue(t, carry):
        for kk in range(TOP_K):
            row_copy(t, kk, slot_ref[0, 0, t * TOP_K + kk]).start()
        return carry

    def drain(t, carry):
        for kk in range(TOP_K):
            row_copy(t, kk, slot_ref[0, 0, t * TOP_K + kk]).wait()
        return carry

    lax.fori_loop(0, tm, issue, 0)
    lax.fori_loop(0, tm, drain, 0)
    gate = gate_ref[...]
    acc = gate[:, 0:1] * ybuf[0]
    for kk in range(1, TOP_K):
        acc = acc + gate[:, kk:kk + 1] * ybuf[kk]
    out = xn_ref[...] + _mod_row(mod_ref, cnd, 5) * acc
    if final:
        out = _rms(out, fin_ref[...])
    o_ref[...] = out


def _combine_call(l, slot, xn, gate, mod, final_norm, ys, cond_base, rows_per_cond, final):
    n = xn.shape[0]
    tm = ROW_TILE
    slot3 = slot.reshape(n // tm, 1, tm * TOP_K)
    kern = functools.partial(_combine_kernel, cond_base=cond_base, rows_per_cond=rows_per_cond,
                             tm=tm, final=final)
    return pl.pallas_call(
        kern,
        grid=(n // tm,),
        in_specs=[pl.BlockSpec((1, 1, tm * TOP_K), lambda i: (i, 0, 0), memory_space=pltpu.SMEM),
                  pl.BlockSpec((tm, D), lambda i: (i, 0)),
                  pl.BlockSpec((tm, TOP_K), lambda i: (i, 0)),
                  pl.BlockSpec((None, 8, N_MOD * D), lambda i: (l, 0, 0)),
                  pl.BlockSpec((1, D), lambda i: (0, 0)),
                  pl.BlockSpec(memory_space=pl.ANY)],
        out_specs=pl.BlockSpec((tm, D), lambda i: (i, 0)),
        out_shape=jax.ShapeDtypeStruct((n, D), F32),
        scratch_shapes=[pltpu.VMEM((TOP_K, tm, D), F32), pltpu.SemaphoreType.DMA(())],
        compiler_params=_cp(("arbitrary",)),
        name="moe_combine",
    )(slot3, xn, gate, mod, final_norm.reshape(1, D), ys)


def kernel(x_prompt, x_sample, cache_k, cache_v, c, c_ctx, ada_w, ada_b, norm_mix, norm_ffn, w_in,
           hy_conv_w, hy_conv_b, hy_ffn_w1, hy_ffn_b1, hy_ffn_w2, hy_ffn_b2, hy_ffn_w3, hy_ffn_b3,
           hy_sin_freq, hy_decay, hy_skip, w_hy_out, w_fn_out, w_na_out, na_rel_bias, w_out,
           router_w, router_b, moe_w1, moe_b1, moe_w2, moe_b2, final_norm):
    Bp, Lp, _ = x_prompt.shape
    Bs, Ls, _ = x_sample.shape
    P = cache_k.shape[2]
    assert Bs + 1 <= 8 and Lp % ROW_TILE == 0 and Ls % ROW_TILE == 0 and Ls % GRID_W == 0
    n_p, n_s = Bp * Lp, Bs * Ls
    hyp = dict(hy_conv_w=hy_conv_w, hy_conv_b=hy_conv_b, hy_ffn_w1=hy_ffn_w1, hy_ffn_b1=hy_ffn_b1,
               hy_ffn_w2=hy_ffn_w2, hy_ffn_b2=hy_ffn_b2, hy_ffn_w3=hy_ffn_w3, hy_ffn_b3=hy_ffn_b3,
               hy_sin_freq=hy_sin_freq, hy_decay=hy_decay, hy_skip=hy_skip)

    r3 = lambda a: a.reshape(DEPTH, 1, a.shape[-1])
    wts = dict(
        norm_mix=r3(norm_mix), norm_ffn=r3(norm_ffn),
        w_g=w_in[:, :, BR_W:].astype(BF16), w_hy_out=w_hy_out.astype(BF16),
        w_fn_out=w_fn_out.astype(BF16), w_na_out=w_na_out.astype(BF16), w_out=w_out.astype(BF16),
        router_w=router_w, router_b=r3(router_b),
        w1g=moe_w1[..., 0::2].astype(BF16), w1l=moe_w1[..., 1::2].astype(BF16),
        b1g=moe_b1[..., 0::2].reshape(DEPTH, N_EXP, 1, FF),
        b1l=moe_b1[..., 1::2].reshape(DEPTH, N_EXP, 1, FF),
        w2=moe_w2.astype(BF16), b2=moe_b2.reshape(DEPTH, N_EXP, 1, D),
    )
    w_in_b = w_in[:, :, :BR_W].astype(BF16)

    cond8 = jnp.zeros((8, D), F32).at[0].set(c_ctx).at[1:1 + Bs].set(c)
    mod = _mod_call(cond8, ada_w, ada_b)

    streams = []
    for (B, L, base, rpc) in ((Bp, Lp, 0, Bp * Lp), (Bs, Ls, 1, Ls)):
        hy_tabs = _hyena_tables(L)
        hc, hs = _hyena_filters(L, hy_tabs, hyp)
        streams.append(dict(B=B, L=L, base=base, rpc=rpc, hy_tabs=hy_tabs, hc=hc, hs=hs,
                            fn_tabs=_fnet_tables(L)))
    rows_s = Ls // GRID_W
    bias_tab = _nbr_bias_table(na_rel_bias, min(WIN_R, rows_s))
    ck = cache_k.reshape(Bs, DEPTH, P, NA_W)
    cv = cache_v.reshape(Bs, DEPTH, P, NA_W)

    n_assign = (n_p + n_s) * TOP_K
    n_blocks = -(-n_assign // MOE_ROWS) + N_EXP
    n_slots = n_blocks * MOE_ROWS

    xs_tok = [x_prompt.reshape(n_p, D), x_sample.reshape(n_s, D)]
    new_k, new_v = [], []
    for l in range(DEPTH):
        merged = []
        cnt = jnp.zeros((1, N_EXP), F32)
        for si, st in enumerate(streams):
            B, L = st["B"], st["L"]
            x = xs_tok[si]
            u_hy, u_fn, q, k, v = _inproj_call(l, x, mod, wts["norm_mix"], w_in_b, st["base"], st["rpc"])
            y_hy = _hyena_call(l, u_hy, B, L, st["hy_tabs"], st["hc"], st["hs"], hyp)
            y_fn = _fnet_call(u_fn, B, L, st["fn_tabs"])
            if si == 0:
                y_na = _ctx_attn_call(q, k, v, B, L)
                new_k.append(k.reshape(B, L, NA_H, NA_D))
                new_v.append(v.reshape(B, L, NA_H, NA_D))
            else:
                y_na = _nbr_attn_call(l, q, k, v, ck, cv, bias_tab, B, L)
            xn, h2, idx, gate, rank, cnt = _merge_call(l, x, y_hy, y_fn, y_na, mod, wts, cnt,
                                                       st["base"], st["rpc"])
            merged.append((xn, h2, idx, gate, rank))

        counts = cnt[0].astype(I32)
        blocks_per_e = (counts + MOE_ROWS - 1) // MOE_ROWS
        block_end = jnp.cumsum(blocks_per_e)
        block_start = block_end - blocks_per_e
        block_expert = jnp.minimum(
            jnp.searchsorted(block_end, jnp.arange(n_blocks, dtype=I32), side="right"),
            N_EXP - 1).astype(I32)
        n_used = block_end[-1:].astype(I32)

        xs = jnp.zeros((n_slots, D), F32)
        slots = []
        for (xn, h2, idx, gate, rank) in merged:
            slot = block_start[idx] * MOE_ROWS + rank
            slots.append(slot)
            xs = _dispatch_call(slot, h2, xs)
        ys = _moe_call(l, xs, block_expert, n_used, wts)
        for si, st in enumerate(streams):
            xn, h2, idx, gate, rank = merged[si]
            xs_tok[si] = _combine_call(l, slots[si], xn, gate, mod, final_norm, ys,
                                       st["base"], st["rpc"], l == DEPTH - 1)

    y_prompt = xs_tok[0].reshape(Bp, Lp, D)
    y_sample = xs_tok[1].reshape(Bs, Ls, D)
    return (y_prompt, y_sample, jnp.stack(new_k, axis=1), jnp.stack(new_v, axis=1))
```

```python
import functools
import math

import numpy as np
import jax
import jax.numpy as jnp
from jax import lax
from jax.experimental import pallas as pl
from jax.experimental.pallas import tpu as pltpu

F32 = jnp.float32
BF16 = jnp.bfloat16
I32 = jnp.int32

D = 1024
DEPTH = 4
N_MOD = 6
RMS_EPS = 1e-6
HY_W = 384
HY_EMB = 33
HY_EMB_PAD = 128
HY_FF = 64
FN_W = 256
FN_GROUPS = 4
FN_GROUP_DIM = 64
NA_H = 6
NA_D = 64
NA_W = NA_H * NA_D
GRID_W = 64
WIN_R = 8
WIN_C = 16
NBR_GROUP = 4
ATTN_SCALE = NA_D ** -0.5
BR_W = 3 * HY_W + FN_W + 3 * NA_W
N_EXP = 32
TOP_K = 4
FF = 1024
SWIGLU_LIMIT = 7.0
SWIGLU_ALPHA = 1.702
MASK_NEG = -1e30
DEINT = 128

ROW_TILE = 256
MERGE_TILE = 512
MOE_ROWS = 512
DFT_TILE = 512
VMEM_LIMIT = 56 * 1024 * 1024


def _cp(sem, vmem=VMEM_LIMIT):
    return pltpu.CompilerParams(dimension_semantics=sem, vmem_limit_bytes=vmem)


def _dot(a, b):
    return jnp.dot(a, b, preferred_element_type=F32)


def _dot_hi(a, b):
    return jnp.dot(a, b, preferred_element_type=F32, precision=lax.Precision.HIGHEST)


def _dot_nt(a, b):
    return lax.dot_general(a, b, (((1,), (1,)), ((), ())), preferred_element_type=F32)


def _mod_row(mod_ref, cnd, k):
    return mod_ref[pl.ds(cnd, 1), pl.ds(k * D, D)]


def _rms(x, g):
    return x * lax.rsqrt(jnp.mean(x * x, axis=-1, keepdims=True) + RMS_EPS) * g


def _mod_kernel(cond_ref, w_ref, b_ref, o_ref):
    c = cond_ref[...]
    s = c * jax.nn.sigmoid(c)
    o_ref[...] = _dot(s.astype(BF16), w_ref[...].astype(BF16)) + b_ref[...]


def _mod_call(cond8, ada_w, ada_b):
    tn = 1536
    return pl.pallas_call(
        _mod_kernel,
        grid=(DEPTH, N_MOD * D // tn),
        in_specs=[pl.BlockSpec((8, D), lambda l, j: (0, 0)),
                  pl.BlockSpec((None, D, tn), lambda l, j: (l, 0, j)),
                  pl.BlockSpec((None, 1, tn), lambda l, j: (l, 0, j))],
        out_specs=pl.BlockSpec((None, 8, tn), lambda l, j: (l, 0, j)),
        out_shape=jax.ShapeDtypeStruct((DEPTH, 8, N_MOD * D), F32),
        compiler_params=_cp(("parallel", "parallel")),
        name="adaln_mod",
    )(cond8, ada_w, ada_b.reshape(DEPTH, 1, N_MOD * D))


def _inproj_kernel(x_ref, mod_ref, g_ref, w_ref, hy_ref, fn_ref, q_ref, k_ref, v_ref, *,
                   cond_base, rows_per_cond, tm):
    cnd = cond_base + (pl.program_id(0) * tm) // rows_per_cond
    h = _rms(x_ref[...], g_ref[...]) * (1.0 + _mod_row(mod_ref, cnd, 1)) + _mod_row(mod_ref, cnd, 0)
    u = _dot(h.astype(BF16), w_ref[...])
    o1 = 3 * HY_W
    o2 = o1 + FN_W
    o3 = o2 + NA_W
    o4 = o3 + NA_W
    hy_ref[...] = u[:, :o1].astype(BF16)
    fn_ref[...] = u[:, o1:o2].astype(BF16)
    q_ref[...] = u[:, o2:o3].astype(BF16)
    k_ref[...] = u[:, o3:o4]
    v_ref[...] = u[:, o4:]


def _inproj_call(l, x, mod, norm_mix, w_in_b, cond_base, rows_per_cond):
    n = x.shape[0]
    tm = MERGE_TILE
    kern = functools.partial(_inproj_kernel, cond_base=cond_base, rows_per_cond=rows_per_cond, tm=tm)
    widths = (3 * HY_W, FN_W, NA_W, NA_W, NA_W)
    dts = (BF16, BF16, BF16, F32, F32)
    return pl.pallas_call(
        kern,
        grid=(n // tm,),
        in_specs=[pl.BlockSpec((tm, D), lambda i: (i, 0)),
                  pl.BlockSpec((None, 8, N_MOD * D), lambda i: (l, 0, 0)),
                  pl.BlockSpec((None, 1, D), lambda i: (l, 0, 0)),
                  pl.BlockSpec((None, D, BR_W), lambda i: (l, 0, 0))],
        out_specs=[pl.BlockSpec((tm, w), lambda i: (i, 0)) for w in widths],
        out_shape=[jax.ShapeDtypeStruct((n, w), dt) for w, dt in zip(widths, dts)],
        compiler_params=_cp(("parallel",)),
        name="in_proj",
    )(x, mod, norm_mix, w_in_b)


def _cs_tables(L, N):
    k = np.arange(L, dtype=np.int64)[:, None]
    na = np.arange(0, L, 64, dtype=np.int64)[None, :]
    nb = np.arange(64, dtype=np.int64)[None, :]
    ang_a = 2.0 * np.pi * ((k * na) % N).astype(np.float64) / N
    ang_b = 2.0 * np.pi * ((k * nb) % N).astype(np.float64) / N
    ca = jnp.asarray(np.cos(ang_a), F32)[:, :, None]
    sa = jnp.asarray(np.sin(ang_a), F32)[:, :, None]
    cb = jnp.asarray(np.cos(ang_b), F32)[:, None, :]
    sb = jnp.asarray(np.sin(ang_b), F32)[:, None, :]
    cos = (ca * cb - sa * sb).reshape(L, L)
    sin = (sa * cb + ca * sb).reshape(L, L)
    return cos, sin


def _hyena_tables(L):
    cos, sin = _cs_tables(L, 2 * L)
    alt = (1 - 2 * (jnp.arange(L) % 2)).astype(F32)
    row0 = (jnp.arange(L) == 0)
    s_fwd = jnp.where(row0[:, None], alt[None, :], sin)
    s_inv = jnp.where(row0[None, :], alt[:, None], sin)
    return cos.astype(BF16), s_fwd.astype(BF16), s_inv.astype(BF16)


def _fnet_tables(L):
    cos, sin = _cs_tables(L, L)
    g = FN_GROUP_DIM
    kk = np.arange(g)[:, None] * np.arange(g)[None, :]
    ang = 2.0 * np.pi * (kk % g) / g
    scale = 1.0 / math.sqrt(L * g)
    cb = np.kron(np.eye(FN_GROUPS), np.cos(ang)) * scale
    sb = np.kron(np.eye(FN_GROUPS), np.sin(ang)) * scale
    csb = jnp.asarray(np.concatenate([cb, sb], axis=1), F32).astype(BF16)
    return cos.astype(BF16), sin.astype(BF16), csb


def _hy_filt_a_kernel(z_ref, w1_ref, b1_ref, w2_ref, b2_ref, w3a_ref, w3b_ref, b3a_ref, b3b_ref,
                      fr_ref, dec_ref, fp_ref, fm_ref, last_ref, nyq_ref, pad_ref, *, L):
    z = z_ref[...]
    fr = fr_ref[...]
    a = jnp.sin(fr * (_dot_hi(z, w1_ref[...]) + b1_ref[...]))
    a = jnp.sin(fr * (_dot_hi(a, w2_ref[...]) + b2_ref[...]))
    window = jnp.exp(-z[:, 0:1] * jnp.abs(dec_ref[...]))
    f0 = (_dot_hi(a, w3a_ref[...]) + b3a_ref[...]) * window
    f1 = (_dot_hi(a, w3b_ref[...]) + b3b_ref[...]) * window
    norm = (jnp.sum(jnp.abs(f0), axis=0, keepdims=True)
            + jnp.sum(jnp.abs(f1), axis=0, keepdims=True) + 1e-6)
    f0 = f0 / norm
    f1 = f1 / norm
    pad_ref[pl.ds(0, 8), :] = jnp.zeros((8, 128), F32)
    pad_ref[pl.ds(8, L), :] = f1
    f1s = pad_ref[pl.ds(7, L), :]
    last = pad_ref[pl.ds(L + 7, 1), :]
    fp = f0 + f1s
    fp_ref[...] = fp.astype(BF16)
    fm_ref[...] = (f0 - f1s).astype(BF16)
    last_ref[...] = last
    row = lax.broadcasted_iota(I32, (L, 128), 0)
    alt = (1 - 2 * (row & 1)).astype(F32)
    nyq_ref[...] = jnp.sum(alt * fp, axis=0, keepdims=True) + last


def _hy_filt_b_kernel(c_ref, s_ref, fp_ref, fm_ref, last_ref, nyq_ref, hc_ref, hs_ref, *, tk):
    kt = pl.program_id(0)
    row = kt * tk + lax.broadcasted_iota(I32, (tk, HY_W), 0)
    alt = (1 - 2 * (row & 1)).astype(F32)
    hc_ref[...] = _dot(c_ref[...], fp_ref[...]) + alt * last_ref[...]
    hs = _dot(s_ref[...], fm_ref[...])
    hs_ref[...] = jnp.where(row == 0, nyq_ref[...], hs)


def _hyena_filters(L, tabs, p):
    cos, s_fwd, _ = tabs
    t = np.linspace(0.0, 1.0, L, dtype=np.float32)[:, None]
    bands = (HY_EMB - 1) // 2
    omega = (2.0 * math.pi * np.arange(L, dtype=np.float32)[:, None] / L).astype(np.float32)
    f = np.linspace(1e-4, bands - 1, bands, dtype=np.float32)[None, :]
    z = np.concatenate([t, np.cos(f * omega), -np.sin(f * omega)], axis=-1).astype(np.float32)
    z = jnp.asarray(np.pad(z, ((0, 0), (0, HY_EMB_PAD - HY_EMB))))
    w1 = jnp.pad(p["hy_ffn_w1"], ((0, 0), (0, HY_EMB_PAD - HY_EMB), (0, 0)))
    r3 = lambda a: a.reshape(DEPTH, 1, a.shape[-1])
    nch = HY_W // 128
    vec = lambda: pl.BlockSpec((None, 1, HY_FF), lambda l, c: (l, 0, 0))
    fp, fm, last, nyq = pl.pallas_call(
        functools.partial(_hy_filt_a_kernel, L=L),
        grid=(DEPTH, nch),
        in_specs=[pl.BlockSpec((L, HY_EMB_PAD), lambda l, c: (0, 0)),
                  pl.BlockSpec((None, HY_EMB_PAD, HY_FF), lambda l, c: (l, 0, 0)), vec(),
                  pl.BlockSpec((None, HY_FF, HY_FF), lambda l, c: (l, 0, 0)), vec(),
                  pl.BlockSpec((None, HY_FF, 128), lambda l, c: (l, 0, c)),
                  pl.BlockSpec((None, HY_FF, 128), lambda l, c: (l, 0, nch + c)),
                  pl.BlockSpec((None, 1, 128), lambda l, c: (l, 0, c)),
                  pl.BlockSpec((None, 1, 128), lambda l, c: (l, 0, nch + c)),
                  vec(),
                  pl.BlockSpec((None, 1, 128), lambda l, c: (l, 0, c))],
        out_specs=[pl.BlockSpec((None, L, 128), lambda l, c: (l, 0, c)),
                   pl.BlockSpec((None, L, 128), lambda l, c: (l, 0, c)),
                   pl.BlockSpec((None, 1, 128), lambda l, c: (l, 0, c)),
                   pl.BlockSpec((None, 1, 128), lambda l, c: (l, 0, c))],
        out_shape=[jax.ShapeDtypeStruct((DEPTH, L, HY_W), BF16),
                   jax.ShapeDtypeStruct((DEPTH, L, HY_W), BF16),
                   jax.ShapeDtypeStruct((DEPTH, 1, HY_W), F32),
                   jax.ShapeDtypeStruct((DEPTH, 1, HY_W), F32)],
        scratch_shapes=[pltpu.VMEM((L + 8, 128), F32)],
        compiler_params=_cp(("parallel", "parallel")),
        name="hyena_filter_taps",
    )(z, w1, r3(p["hy_ffn_b1"]), p["hy_ffn_w2"], r3(p["hy_ffn_b2"]), p["hy_ffn_w3"], p["hy_ffn_w3"],
      r3(p["hy_ffn_b3"]), r3(p["hy_ffn_b3"]), r3(p["hy_sin_freq"]), r3(p["hy_decay"]))
    tk = min(DFT_TILE, L)
    full = lambda: pl.BlockSpec((None, L, HY_W), lambda kt, l: (l, 0, 0))
    one = lambda: pl.BlockSpec((None, 1, HY_W), lambda kt, l: (l, 0, 0))
    hc, hs = pl.pallas_call(
        functools.partial(_hy_filt_b_kernel, tk=tk),
        grid=(L // tk, DEPTH),
        in_specs=[pl.BlockSpec((tk, L), lambda kt, l: (kt, 0)),
                  pl.BlockSpec((tk, L), lambda kt, l: (kt, 0)),
                  full(), full(), one(), one()],
        out_specs=[pl.BlockSpec((None, tk, HY_W), lambda kt, l: (l, kt, 0)),
                   pl.BlockSpec((None, tk, HY_W), lambda kt, l: (l, kt, 0))],
        out_shape=[jax.ShapeDtypeStruct((DEPTH, L, HY_W), F32)] * 2,
        compiler_params=_cp(("parallel", "parallel")),
        name="hyena_filter_dft",
    )(cos, s_fwd, fp, fm, last, nyq)
    return hc, hs


def _hy_pre_kernel(u0_ref, u1_ref, u2_ref, w0_ref, w1_ref, w2_ref, b0_ref, b1_ref, b2_ref,
                   zz_ref, x0_ref, pad_ref, *, L):
    zeros8 = jnp.zeros((8, 128), F32)
    pad_ref[pl.ds(0, 8), :] = zeros8
    pad_ref[pl.ds(L + 8, 8), :] = zeros8

    def conv(u_ref, w_ref, b_ref):
        pad_ref[pl.ds(8, L), :] = u_ref[...].astype(F32)
        w = w_ref[...]
        return (w[0:1] * pad_ref[pl.ds(7, L), :] + w[1:2] * pad_ref[pl.ds(8, L), :]
                + w[2:3] * pad_ref[pl.ds(9, L), :] + b_ref[...])

    x0_ref[...] = conv(u0_ref, w0_ref, b0_ref).astype(BF16)
    x1 = conv(u1_ref, w1_ref, b1_ref)
    v = conv(u2_ref, w2_ref, b2_ref)
    zz_ref[...] = (x1 * v).astype(BF16)


def _hy_fwd_kernel(c_ref, s_ref, zz_ref, hc_ref, hs_ref, pr_ref, pi_ref, *, tk, L):
    kt = pl.program_id(0)
    zz = zz_ref[...]
    zc = _dot(c_ref[...], zz)
    zs = _dot(s_ref[...], zz)
    hc = hc_ref[...]
    hs = hs_ref[...]
    row = kt * tk + lax.broadcasted_iota(I32, (tk, HY_W), 0)
    first = row == 0
    wgt = jnp.where(first, 1.0 / (2 * L), 2.0 / (2 * L)).astype(F32)
    ss = zs * hs
    pr_ref[...] = ((zc * hc - jnp.where(first, 0.0, ss)) * wgt).astype(BF16)
    pi_ref[...] = (jnp.where(first, ss, zc * hs + zs * hc) * wgt).astype(BF16)


def _hy_inv_kernel(c_ref, st_ref, pr_ref, pi_ref, zz_ref, x0_ref, skip_ref, o_ref):
    y = _dot(c_ref[...], pr_ref[...]) + _dot(st_ref[...], pi_ref[...])
    zz = zz_ref[...].astype(F32)
    o_ref[...] = (x0_ref[...].astype(F32) * (y + zz * skip_ref[...])).astype(BF16)


def _hyena_call(l, u_hy, B, L, tabs, hc, hs, p):
    cos, s_fwd, s_inv = tabs
    n = B * L
    nch = HY_W // 128
    cw = p["hy_conv_w"].reshape(DEPTH, 3, 3 * HY_W)
    cb = p["hy_conv_b"].reshape(DEPTH, 1, 3 * HY_W)
    ub = lambda part: pl.BlockSpec((L, 128), lambda b, c: (b, part * nch + c))
    wb = lambda part: pl.BlockSpec((None, 3, 128), lambda b, c: (l, 0, part * nch + c))
    bb = lambda part: pl.BlockSpec((None, 1, 128), lambda b, c: (l, 0, part * nch + c))
    zz, x0 = pl.pallas_call(
        functools.partial(_hy_pre_kernel, L=L),
        grid=(B, nch),
        in_specs=[ub(0), ub(1), ub(2), wb(0), wb(1), wb(2), bb(0), bb(1), bb(2)],
        out_specs=[pl.BlockSpec((L, 128), lambda b, c: (b, c))] * 2,
        out_shape=[jax.ShapeDtypeStruct((n, HY_W), BF16)] * 2,
        scratch_shapes=[pltpu.VMEM((L + 16, 128), F32)],
        compiler_params=_cp(("parallel", "parallel")),
        name="hyena_short_conv",
    )(u_hy, u_hy, u_hy, cw, cw, cw, cb, cb, cb)

    tk = min(DFT_TILE, L)
    nk = L // tk
    pr, pi = pl.pallas_call(
        functools.partial(_hy_fwd_kernel, tk=tk, L=L),
        grid=(nk, B),
        in_specs=[pl.BlockSpec((tk, L), lambda kt, b: (kt, 0)),
                  pl.BlockSpec((tk, L), lambda kt, b: (kt, 0)),
                  pl.BlockSpec((L, HY_W), lambda kt, b: (b, 0)),
                  pl.BlockSpec((None, tk, HY_W), lambda kt, b: (l, kt, 0)),
                  pl.BlockSpec((None, tk, HY_W), lambda kt, b: (l, kt, 0))],
        out_specs=[pl.BlockSpec((tk, HY_W), lambda kt, b: (b * nk + kt, 0))] * 2,
        out_shape=[jax.ShapeDtypeStruct((n, HY_W), BF16)] * 2,
        compiler_params=_cp(("parallel", "parallel")),
        name="hyena_dft_fwd",
    )(cos, s_fwd, zz, hc, hs)

    y = pl.pallas_call(
        _hy_inv_kernel,
        grid=(nk, B),
        in_specs=[pl.BlockSpec((tk, L), lambda qt, b: (qt, 0)),
                  pl.BlockSpec((tk, L), lambda qt, b: (qt, 0)),
                  pl.BlockSpec((L, HY_W), lambda qt, b: (b, 0)),
                  pl.BlockSpec((L, HY_W), lambda qt, b: (b, 0)),
                  pl.BlockSpec((tk, HY_W), lambda qt, b: (b * nk + qt, 0)),
                  pl.BlockSpec((tk, HY_W), lambda qt, b: (b * nk + qt, 0)),
                  pl.BlockSpec((None, 1, HY_W), lambda qt, b: (l, 0, 0))],
        out_specs=pl.BlockSpec((tk, HY_W), lambda qt, b: (b * nk + qt, 0)),
        out_shape=jax.ShapeDtypeStruct((n, HY_W), BF16),
        compiler_params=_cp(("parallel", "parallel")),
        name="hyena_dft_inv",
    )(cos, s_inv, pr, pi, zz, x0, p["hy_skip"].reshape(DEPTH, 1, HY_W))
    return y


def _fnet_kernel(u_ref, csb_ref, c_ref, s_ref, o_ref, uc_ref, us_ref):
    @pl.when(pl.program_id(1) == 0)
    def _():
        t = _dot(u_ref[...], csb_ref[...])
        uc_ref[...] = t[:, :FN_W].astype(BF16)
        us_ref[...] = t[:, FN_W:].astype(BF16)

    o_ref[...] = (_dot(c_ref[...], uc_ref[...]) - _dot(s_ref[...], us_ref[...])).astype(BF16)


def _fnet_call(u_fn, B, L, tabs):
    cos, sin, csb = tabs
    tq = min(DFT_TILE, L)
    nq = L // tq
    return pl.pallas_call(
        _fnet_kernel,
        grid=(B, nq),
        in_specs=[pl.BlockSpec((L, FN_W), lambda b, qt: (b, 0)),
                  pl.BlockSpec((FN_W, 2 * FN_W), lambda b, qt: (0, 0)),
                  pl.BlockSpec((tq, L), lambda b, qt: (qt, 0)),
                  pl.BlockSpec((tq, L), lambda b, qt: (qt, 0))],
        out_specs=pl.BlockSpec((tq, FN_W), lambda b, qt: (b * nq + qt, 0)),
        out_shape=jax.ShapeDtypeStruct((B * L, FN_W), BF16),
        scratch_shapes=[pltpu.VMEM((L, FN_W), BF16), pltpu.VMEM((L, FN_W), BF16)],
        compiler_params=_cp(("parallel", "arbitrary")),
        name="fnet_dft2",
    )(u_fn, csb, cos, sin)


def _ctx_attn_kernel(q_ref, k_ref, v_ref, o_ref):
    for h in range(NA_H):
        sl = slice(h * NA_D, (h + 1) * NA_D)
        q = q_ref[:, sl]
        k = k_ref[:, sl].astype(BF16)
        v = v_ref[:, sl].astype(BF16)
        s = _dot_nt(q, k) * ATTN_SCALE
        m = jnp.max(s, axis=-1, keepdims=True)
        e = jnp.exp(s - m)
        den = jnp.sum(e, axis=-1, keepdims=True)
        o_ref[:, sl] = (_dot(e.astype(BF16), v) / den).astype(BF16)


def _ctx_attn_call(q, k, v, B, L):
    blk = lambda: pl.BlockSpec((L, NA_W), lambda b: (b, 0))
    return pl.pallas_call(
        _ctx_attn_kernel,
        grid=(B,),
        in_specs=[blk(), blk(), blk()],
        out_specs=blk(),
        out_shape=jax.ShapeDtypeStruct((B * L, NA_W), BF16),
        compiler_params=_cp(("parallel",)),
        name="context_attention",
    )(q, k, v)


def _nbr_attn_kernel(q_ref, k_ref, v_ref, kc_ref, vc_ref, bias_ref, o_ref, *, rows, band_rows):
    r0 = pl.program_id(1) * NBR_GROUP
    bs = jnp.clip(r0 - WIN_R // 2, 0, rows - band_rows)
    start = pl.multiple_of(bs * GRID_W, GRID_W)
    band = band_rows * GRID_W
    for h in range(NA_H):
        sl = slice(h * NA_D, (h + 1) * NA_D)
        q = q_ref[:, sl]
        kb = k_ref[pl.ds(start, band), sl].astype(BF16)
        vb = v_ref[pl.ds(start, band), sl].astype(BF16)
        kc = kc_ref[:, sl].astype(BF16)
        vc = vc_ref[:, sl].astype(BF16)
        s_loc = _dot_nt(q, kb) * ATTN_SCALE + bias_ref[h]
        s_ctx = _dot_nt(q, kc) * ATTN_SCALE
        m = jnp.maximum(jnp.max(s_loc, axis=-1, keepdims=True), jnp.max(s_ctx, axis=-1, keepdims=True))
        e_loc = jnp.exp(s_loc - m)
        e_ctx = jnp.exp(s_ctx - m)
        den = jnp.sum(e_loc, axis=-1, keepdims=True) + jnp.sum(e_ctx, axis=-1, keepdims=True)
        o = _dot(e_loc.astype(BF16), vb) + _dot(e_ctx.astype(BF16), vc)
        o_ref[:, sl] = (o / den).astype(BF16)


def _nbr_bias_table(rel_bias, rows):
    wr = min(WIN_R, rows)
    band_rows = min(rows, wr + NBR_GROUP - 1)
    n_groups = rows // NBR_GROUP
    cols = np.arange(GRID_W)
    cs = np.clip(cols - WIN_C // 2, 0, GRID_W - WIN_C)
    mask = (cols[None, :] >= cs[:, None]) & (cols[None, :] < cs[:, None] + WIN_C)
    dc = np.clip(cols[None, :] - cols[:, None], -(WIN_C - 1), WIN_C - 1) + WIN_C - 1
    onehot = ((dc[None] == np.arange(2 * WIN_C - 1)[:, None, None]) & mask[None]).astype(np.float32)
    t = jnp.einsum("lhdj,jqk->lhdqk", rel_bias.astype(F32), jnp.asarray(onehot),
                   precision=lax.Precision.HIGHEST)
    t = t + jnp.asarray(np.where(mask, 0.0, MASK_NEG).astype(np.float32))
    masked = jnp.full(t.shape[:2] + t.shape[3:], MASK_NEG, F32)
    kinds = []
    for g in sorted({0, min(1, n_groups - 1), n_groups - 1}):
        r0 = g * NBR_GROUP
        bs = int(np.clip(r0 - wr // 2, 0, rows - band_rows))
        q_rows = []
        for i in range(NBR_GROUP):
            r = r0 + i
            rs = int(np.clip(r - wr // 2, 0, rows - wr))
            blocks = []
            for u in range(band_rows):
                key_row = bs + u
                blocks.append(t[:, :, key_row - r + WIN_R - 1] if rs <= key_row < rs + wr else masked)
            q_rows.append(jnp.concatenate(blocks, axis=-1))
        kinds.append(jnp.concatenate(q_rows, axis=-2))
    while len(kinds) < 3:
        kinds.append(kinds[-1])
    return jnp.stack(kinds, axis=1), band_rows


def _nbr_attn_call(l, q, k, v, ck, cv, bias, band_rows, B, L):
    rows = L // GRID_W
    n_groups = rows // NBR_GROUP
    P = ck.shape[2]
    gq = NBR_GROUP * GRID_W

    def kind(g):
        return jnp.where(g == 0, 0, jnp.where(g == n_groups - 1, 2, 1))

    return pl.pallas_call(
        functools.partial(_nbr_attn_kernel, rows=rows, band_rows=band_rows),
        grid=(B, n_groups),
        in_specs=[pl.BlockSpec((gq, NA_W), lambda b, g: (b * n_groups + g, 0)),
                  pl.BlockSpec((L, NA_W), lambda b, g: (b, 0)),
                  pl.BlockSpec((L, NA_W), lambda b, g: (b, 0)),
                  pl.BlockSpec((None, None, P, NA_W), lambda b, g: (b, l, 0, 0)),
                  pl.BlockSpec((None, None, P, NA_W), lambda b, g: (b, l, 0, 0)),
                  pl.BlockSpec((None, None, NA_H, gq, band_rows * GRID_W),
                               lambda b, g: (l, kind(g), 0, 0, 0))],
        out_specs=pl.BlockSpec((gq, NA_W), lambda b, g: (b * n_groups + g, 0)),
        out_shape=jax.ShapeDtypeStruct((B * L, NA_W), BF16),
        compiler_params=_cp(("parallel", "parallel")),
        name="neighbourhood_attention",
    )(q, k, v, ck, cv, bias)


def _merge_kernel(x_ref, yhy_ref, yfn_ref, yna_ref, mod_ref, gmix_ref, gffn_ref, wg_ref, why_ref,
                  wfn_ref, wna_ref, wout_ref, rw_ref, rb_ref, cnt0_ref,
                  xn_ref, h2_ref, idx_ref, gate_ref, rank_ref, cnt_ref, run_ref, *,
                  cond_base, rows_per_cond, tm):
    i = pl.program_id(0)
    cnd = cond_base + (i * tm) // rows_per_cond

    @pl.when(i == 0)
    def _():
        run_ref[...] = cnt0_ref[...]

    x = x_ref[...]
    h = (_rms(x, gmix_ref[...]) * (1.0 + _mod_row(mod_ref, cnd, 1)) + _mod_row(mod_ref, cnd, 0)).astype(BF16)
    merged = None
    for b, (y_ref, w_ref) in enumerate(((yhy_ref, why_ref), (yfn_ref, wfn_ref), (yna_ref, wna_ref))):
        gate_b = jax.nn.sigmoid(_dot(h, wg_ref[:, b * D:(b + 1) * D]))
        term = gate_b * _dot(y_ref[...], w_ref[...])
        merged = term if merged is None else merged + term
    xn = x + _mod_row(mod_ref, cnd, 2) * _dot(merged.astype(BF16), wout_ref[...])
    xn_ref[...] = xn
    h2 = _rms(xn, gffn_ref[...]) * (1.0 + _mod_row(mod_ref, cnd, 4)) + _mod_row(mod_ref, cnd, 3)
    h2_ref[...] = h2

    h2_hi = h2.astype(BF16)
    h2_lo = (h2 - h2_hi.astype(F32)).astype(BF16)
    rw = rw_ref[...]
    rw_hi = rw.astype(BF16)
    rw_lo = (rw - rw_hi.astype(F32)).astype(BF16)
    logits = _dot(h2_hi, rw_hi) + _dot(h2_lo, rw_hi) + _dot(h2_hi, rw_lo) + rb_ref[...]
    lane = lax.broadcasted_iota(I32, (tm, N_EXP), 1)
    lane4 = lax.broadcasted_iota(I32, (tm, TOP_K), 1)
    rem = logits
    vals, idxs = [], []
    for _ in range(TOP_K):
        m = jnp.max(rem, axis=-1, keepdims=True)
        ix = jnp.min(jnp.where(rem == m, lane, N_EXP), axis=-1, keepdims=True)
        vals.append(m)
        idxs.append(ix)
        rem = jnp.where(lane == ix, -jnp.inf, rem)
    es = [jnp.exp(v - vals[0]) for v in vals]
    den = es[0] + es[1] + es[2] + es[3]

    onehot = jnp.zeros((tm, N_EXP), F32)
    for ix in idxs:
        onehot = onehot + (lane == ix).astype(F32)
    r_i = lax.broadcasted_iota(I32, (tm, tm), 0)
    c_i = lax.broadcasted_iota(I32, (tm, tm), 1)
    lower = (c_i < r_i).astype(BF16)
    before = _dot(lower, onehot.astype(BF16)) + run_ref[...]

    idx_o = jnp.zeros((tm, TOP_K), I32)
    gate_o = jnp.zeros((tm, TOP_K), F32)
    rank_o = jnp.zeros((tm, TOP_K), F32)
    for kk in range(TOP_K):
        rk = jnp.sum(jnp.where(lane == idxs[kk], before, 0.0), axis=-1, keepdims=True)
        idx_o = jnp.where(lane4 == kk, idxs[kk], idx_o)
        gate_o = jnp.where(lane4 == kk, es[kk] / den, gate_o)
        rank_o = jnp.where(lane4 == kk, rk, rank_o)
    idx_ref[...] = idx_o
    gate_ref[...] = gate_o
    rank_ref[...] = rank_o.astype(I32)
    run_ref[...] = run_ref[...] + jnp.sum(onehot, axis=0, keepdims=True)
    cnt_ref[...] = run_ref[...]


def _merge_call(l, x, y_hy, y_fn, y_na, mod, wts, cnt0, cond_base, rows_per_cond):
    n = x.shape[0]
    tm = MERGE_TILE
    kern = functools.partial(_merge_kernel, cond_base=cond_base, rows_per_cond=rows_per_cond, tm=tm)
    row = lambda w: pl.BlockSpec((tm, w), lambda i: (i, 0))
    lw = lambda a, b: pl.BlockSpec((None, a, b), lambda i: (l, 0, 0), pipeline_mode=pl.Buffered(1))
    return pl.pallas_call(
        kern,
        grid=(n // tm,),
        in_specs=[row(D), row(HY_W), row(FN_W), row(NA_W),
                  lw(8, N_MOD * D), lw(1, D), lw(1, D),
                  lw(D, 3 * D), lw(HY_W, D), lw(FN_W, D), lw(NA_W, D), lw(D, D),
                  lw(D, N_EXP), lw(1, N_EXP),
                  pl.BlockSpec((1, N_EXP), lambda i: (0, 0))],
        out_specs=[row(D), row(D), row(TOP_K), row(TOP_K), row(TOP_K),
                   pl.BlockSpec((1, N_EXP), lambda i: (0, 0))],
        out_shape=[jax.ShapeDtypeStruct((n, D), F32), jax.ShapeDtypeStruct((n, D), F32),
                   jax.ShapeDtypeStruct((n, TOP_K), I32), jax.ShapeDtypeStruct((n, TOP_K), F32),
                   jax.ShapeDtypeStruct((n, TOP_K), I32), jax.ShapeDtypeStruct((1, N_EXP), F32)],
        scratch_shapes=[pltpu.VMEM((1, N_EXP), F32)],
        compiler_params=_cp(("arbitrary",)),
        name="merge_router",
    )(x, y_hy, y_fn, y_na, mod, wts["norm_mix"], wts["norm_ffn"], wts["w_g"], wts["w_hy_out"],
      wts["w_fn_out"], wts["w_na_out"], wts["w_out"], wts["router_w"], wts["router_b"], cnt0)


def _dispatch_kernel(slot_ref, h_ref, xs_in_ref, xs_ref, sem, *, tm):
    del xs_in_ref

    def row_copy(t, s):
        return pltpu.make_async_copy(h_ref.at[pl.ds(t, 1)], xs_ref.at[pl.ds(s, 1)], sem)

    def issue(t, carry):
        for kk in range(TOP_K):
            row_copy(t, slot_ref[0, 0, t * TOP_K + kk]).start(priority=kk % 2)
        return carry

    def drain(t, carry):
        for kk in range(TOP_K):
            row_copy(t, slot_ref[0, 0, t * TOP_K + kk]).wait()
        return carry

    lax.fori_loop(0, tm, issue, 0)
    lax.fori_loop(0, tm, drain, 0)


def _dispatch_call(slot, h2, xs):
    n = h2.shape[0]
    tm = ROW_TILE
    slot3 = slot.reshape(n // tm, 1, tm * TOP_K)
    return pl.pallas_call(
        functools.partial(_dispatch_kernel, tm=tm),
        grid=(n // tm,),
        in_specs=[pl.BlockSpec((1, 1, tm * TOP_K), lambda i: (i, 0, 0), memory_space=pltpu.SMEM),
                  pl.BlockSpec((tm, D), lambda i: (i, 0)),
                  pl.BlockSpec(memory_space=pl.ANY)],
        out_specs=pl.BlockSpec(memory_space=pl.ANY),
        out_shape=jax.ShapeDtypeStruct(xs.shape, xs.dtype),
        scratch_shapes=[pltpu.SemaphoreType.DMA(())],
        input_output_aliases={2: 0},
        compiler_params=_cp(("arbitrary",)),
        name="moe_dispatch",
    )(slot3, h2, xs)


def _moe_kernel(be_ref, nu_ref, x_ref, w1_ref, b1g_ref, b1l_ref, w2_ref, b2_ref, y_ref,
                w1g_s, w1l_s, w2_s):
    j = pl.program_id(0)
    live = j < nu_ref[0]
    new_expert = jnp.logical_or(j == 0, be_ref[j] != be_ref[jnp.maximum(j - 1, 0)])

    @pl.when(jnp.logical_and(live, new_expert))
    def _():
        grp = 2 * DEINT
        r = lax.broadcasted_iota(I32, (grp, grp), 0)
        c = lax.broadcasted_iota(I32, (grp, grp), 1)
        perm = (r == jnp.where(c < DEINT, 2 * c, 2 * (c - DEINT) + 1)).astype(BF16)
        for ch in range(2 * FF // grp):
            t = _dot(w1_ref[:, ch * grp:(ch + 1) * grp].astype(BF16), perm).astype(BF16)
            w1g_s[:, ch * DEINT:(ch + 1) * DEINT] = t[:, :DEINT]
            w1l_s[:, ch * DEINT:(ch + 1) * DEINT] = t[:, DEINT:]
        w2_s[...] = w2_ref[...].astype(BF16)

    @pl.when(live)
    def _():
        x = x_ref[...].astype(BF16)
        g = jnp.minimum(_dot(x, w1g_s[...]) + b1g_ref[...], SWIGLU_LIMIT)
        lin = jnp.clip(_dot(x, w1l_s[...]) + b1l_ref[...], -SWIGLU_LIMIT, SWIGLU_LIMIT)
        act = g * jax.nn.sigmoid(SWIGLU_ALPHA * g) * (lin + 1.0)
        y_ref[...] = _dot(act.astype(BF16), w2_s[...]) + b2_ref[...]

    @pl.when(jnp.logical_not(live))
    def _():
        y_ref[...] = jnp.zeros_like(y_ref)


def _moe_call(l, xs, block_expert, n_used, wts):
    n_slots = xs.shape[0]
    R = MOE_ROWS
    bspec = lambda: pl.BlockSpec((None, None, 1, FF), lambda j, be, nu: (l, be[j], 0, 0))
    grid_spec = pltpu.PrefetchScalarGridSpec(
        num_scalar_prefetch=2,
        grid=(n_slots // R,),
        in_specs=[pl.BlockSpec((R, D), lambda j, be, nu: (j, 0)),
                  pl.BlockSpec((None, None, D, 2 * FF), lambda j, be, nu: (l, be[j], 0, 0)),
                  bspec(), bspec(),
                  pl.BlockSpec((None, None, FF, D), lambda j, be, nu: (l, be[j], 0, 0)),
                  bspec()],
        out_specs=pl.BlockSpec((R, D), lambda j, be, nu: (j, 0)),
        scratch_shapes=[pltpu.VMEM((D, FF), BF16), pltpu.VMEM((D, FF), BF16),
                        pltpu.VMEM((FF, D), BF16)],
    )
    return pl.pallas_call(
        _moe_kernel,
        grid_spec=grid_spec,
        out_shape=jax.ShapeDtypeStruct((n_slots, D), F32),
        compiler_params=_cp(("arbitrary",)),
        name="moe_experts",
    )(block_expert, n_used, xs, wts["w1"], wts["b1g"], wts["b1l"], wts["w2"], wts["b2"])


def _combine_kernel(slot_ref, xn_ref, gate_ref, mod_ref, fin_ref, ys_ref, o_ref, ybuf, sem, *,
                    cond_base, rows_per_cond, tm, final):
    cnd = cond_base + (pl.program_id(0) * tm) // rows_per_cond

    def row_copy(t, kk, s):
        return pltpu.make_async_copy(ys_ref.at[pl.ds(s, 1)], ybuf.at[kk, pl.ds(t, 1)], sem)

    def issue(t, carry):
        for kk in range(TOP_K):
            row_copy(t, kk, slot_ref[0, 0, t * TOP_K + kk]).start(priority=kk % 2)
        return carry

    def drain(t, carry):
        for kk in range(TOP_K):
            row_copy(t, kk, slot_ref[0, 0, t * TOP_K + kk]).wait()
        return carry

    lax.fori_loop(0, tm, issue, 0)
    lax.fori_loop(0, tm, drain, 0)
    gate = gate_ref[...]
    acc = gate[:, 0:1] * ybuf[0]
    for kk in range(1, TOP_K):
        acc = acc + gate[:, kk:kk + 1] * ybuf[kk]
    out = xn_ref[...] + _mod_row(mod_ref, cnd, 5) * acc
    if final:
        out = _rms(out, fin_ref[...])
    o_ref[...] = out


def _combine_call(l, slot, xn, gate, mod, final_norm, ys, cond_base, rows_per_cond, final):
    n = xn.shape[0]
    tm = ROW_TILE
    slot3 = slot.reshape(n // tm, 1, tm * TOP_K)
    kern = functools.partial(_combine_kernel, cond_base=cond_base, rows_per_cond=rows_per_cond,
                             tm=tm, final=final)
    return pl.pallas_call(
        kern,
        grid=(n // tm,),
        in_specs=[pl.BlockSpec((1, 1, tm * TOP_K), lambda i: (i, 0, 0), memory_space=pltpu.SMEM),
                  pl.BlockSpec((tm, D), lambda i: (i, 0)),
                  pl.BlockSpec((tm, TOP_K), lambda i: (i, 0)),
                  pl.BlockSpec((None, 8, N_MOD * D), lambda i: (l, 0, 0)),
                  pl.BlockSpec((1, D), lambda i: (0, 0)),
                  pl.BlockSpec(memory_space=pl.ANY)],
        out_specs=pl.BlockSpec((tm, D), lambda i: (i, 0)),
        out_shape=jax.ShapeDtypeStruct((n, D), F32),
        scratch_shapes=[pltpu.VMEM((TOP_K, tm, D), F32), pltpu.SemaphoreType.DMA(())],
        compiler_params=_cp(("arbitrary",)),
        name="moe_combine",
    )(slot3, xn, gate, mod, final_norm.reshape(1, D), ys)


def kernel(x_prompt, x_sample, cache_k, cache_v, c, c_ctx, ada_w, ada_b, norm_mix, norm_ffn, w_in,
           hy_conv_w, hy_conv_b, hy_ffn_w1, hy_ffn_b1, hy_ffn_w2, hy_ffn_b2, hy_ffn_w3, hy_ffn_b3,
           hy_sin_freq, hy_decay, hy_skip, w_hy_out, w_fn_out, w_na_out, na_rel_bias, w_out,
           router_w, router_b, moe_w1, moe_b1, moe_w2, moe_b2, final_norm):
    Bp, Lp, _ = x_prompt.shape
    Bs, Ls, _ = x_sample.shape
    P = cache_k.shape[2]
    assert Bs + 1 <= 8 and Lp % ROW_TILE == 0 and Ls % MERGE_TILE == 0 and Ls % GRID_W == 0
    assert (Bp * Lp) % MERGE_TILE == 0
    n_p, n_s = Bp * Lp, Bs * Ls
    hyp = dict(hy_conv_w=hy_conv_w, hy_conv_b=hy_conv_b, hy_ffn_w1=hy_ffn_w1, hy_ffn_b1=hy_ffn_b1,
               hy_ffn_w2=hy_ffn_w2, hy_ffn_b2=hy_ffn_b2, hy_ffn_w3=hy_ffn_w3, hy_ffn_b3=hy_ffn_b3,
               hy_sin_freq=hy_sin_freq, hy_decay=hy_decay, hy_skip=hy_skip)

    r3 = lambda a: a.reshape(DEPTH, 1, a.shape[-1])
    wts = dict(
        norm_mix=r3(norm_mix), norm_ffn=r3(norm_ffn),
        w_g=w_in[:, :, BR_W:].astype(BF16), w_hy_out=w_hy_out.astype(BF16),
        w_fn_out=w_fn_out.astype(BF16), w_na_out=w_na_out.astype(BF16), w_out=w_out.astype(BF16),
        router_w=router_w, router_b=r3(router_b),
        w1=moe_w1,
        b1g=moe_b1[..., 0::2].reshape(DEPTH, N_EXP, 1, FF),
        b1l=moe_b1[..., 1::2].reshape(DEPTH, N_EXP, 1, FF),
        w2=moe_w2, b2=moe_b2.reshape(DEPTH, N_EXP, 1, D),
    )
    w_in_b = w_in[:, :, :BR_W].astype(BF16)

    cond8 = jnp.zeros((8, D), F32).at[0].set(c_ctx).at[1:1 + Bs].set(c)
    mod = _mod_call(cond8, ada_w, ada_b)

    streams = []
    for (B, L, base, rpc) in ((Bp, Lp, 0, Bp * Lp), (Bs, Ls, 1, Ls)):
        hy_tabs = _hyena_tables(L)
        hc, hs = _hyena_filters(L, hy_tabs, hyp)
        streams.append(dict(B=B, L=L, base=base, rpc=rpc, hy_tabs=hy_tabs, hc=hc, hs=hs,
                            fn_tabs=_fnet_tables(L)))
    rows_s = Ls // GRID_W
    assert rows_s % NBR_GROUP == 0 and rows_s >= WIN_R + NBR_GROUP - 1
    bias_tab, band_rows = _nbr_bias_table(na_rel_bias, rows_s)
    ck = cache_k.reshape(Bs, DEPTH, P, NA_W)
    cv = cache_v.reshape(Bs, DEPTH, P, NA_W)

    n_assign = (n_p + n_s) * TOP_K
    n_blocks = -(-n_assign // MOE_ROWS) + N_EXP
    n_slots = n_blocks * MOE_ROWS

    xs_tok = [x_prompt.reshape(n_p, D), x_sample.reshape(n_s, D)]
    new_k, new_v = [], []
    for l in range(DEPTH):
        merged = []
        cnt = jnp.zeros((1, N_EXP), F32)
        for si, st in enumerate(streams):
            B, L = st["B"], st["L"]
            x = xs_tok[si]
            u_hy, u_fn, q, k, v = _inproj_call(l, x, mod, wts["norm_mix"], w_in_b, st["base"], st["rpc"])
            y_hy = _hyena_call(l, u_hy, B, L, st["hy_tabs"], st["hc"], st["hs"], hyp)
            y_fn = _fnet_call(u_fn, B, L, st["fn_tabs"])
            if si == 0:
                y_na = _ctx_attn_call(q, k, v, B, L)
                new_k.append(k.reshape(B, L, NA_H, NA_D))
                new_v.append(v.reshape(B, L, NA_H, NA_D))
            else:
                y_na = _nbr_attn_call(l, q, k, v, ck, cv, bias_tab, band_rows, B, L)
            xn, h2, idx, gate, rank, cnt = _merge_call(l, x, y_hy, y_fn, y_na, mod, wts, cnt,
                                                       st["base"], st["rpc"])
            merged.append((xn, h2, idx, gate, rank))

        counts = cnt[0].astype(I32)
        blocks_per_e = (counts + MOE_ROWS - 1) // MOE_ROWS
        block_end = jnp.cumsum(blocks_per_e)
        block_start = block_end - blocks_per_e
        block_expert = jnp.minimum(
            jnp.sum(jnp.arange(n_blocks, dtype=I32)[:, None] >= block_end[None, :], axis=1),
            N_EXP - 1).astype(I32)
        n_used = block_end[-1:].astype(I32)

        xs = jnp.zeros((n_slots, D), F32)
        slots = []
        for (xn, h2, idx, gate, rank) in merged:
            start_of = jnp.sum(jnp.where(idx[..., None] == jnp.arange(N_EXP, dtype=I32), block_start, 0), axis=-1)
            slot = start_of * MOE_ROWS + rank
            slots.append(slot)
            xs = _dispatch_call(slot, h2, xs)
        ys = _moe_call(l, xs, block_expert, n_used, wts)
        for si, st in enumerate(streams):
            xn, h2, idx, gate, rank = merged[si]
            xs_tok[si] = _combine_call(l, slots[si], xn, gate, mod, final_norm, ys,
                                       st["base"], st["rpc"], l == DEPTH - 1)

    y_prompt = xs_tok[0].reshape(Bp, Lp, D)
    y_sample = xs_tok[1].reshape(Bs, Ls, D)
    return (y_prompt, y_sample, jnp.stack(new_k, axis=1), jnp.stack(new_v, axis=1))
```

```python
import functools
import math

import numpy as np
import jax
import jax.numpy as jnp
from jax import lax
from jax.experimental import pallas as pl
from jax.experimental.pallas import tpu as pltpu

F32 = jnp.float32
BF16 = jnp.bfloat16
I32 = jnp.int32

D = 1024
DEPTH = 4
N_MOD = 6
RMS_EPS = 1e-6
HY_W = 384
HY_EMB = 33
HY_EMB_PAD = 128
HY_FF = 64
FN_W = 256
FN_GROUPS = 4
FN_GROUP_DIM = 64
NA_H = 6
NA_D = 64
NA_W = NA_H * NA_D
GRID_W = 64
WIN_R = 8
WIN_C = 16
NBR_GROUP = 4
ATTN_SCALE = NA_D ** -0.5
BR_W = 3 * HY_W + FN_W + 3 * NA_W
N_EXP = 32
TOP_K = 4
FF = 1024
SWIGLU_LIMIT = 7.0
SWIGLU_ALPHA = 1.702
MASK_NEG = -1e30
DEINT = 128

ROW_TILE = 256
MERGE_TILE = 512
MOE_ROWS = 512
DFT_TILE = 512
VMEM_LIMIT = 56 * 1024 * 1024


def _cp(sem, vmem=VMEM_LIMIT):
    return pltpu.CompilerParams(dimension_semantics=sem, vmem_limit_bytes=vmem)


def _dot(a, b):
    return jnp.dot(a, b, preferred_element_type=F32)


def _dot_hi(a, b):
    return jnp.dot(a, b, preferred_element_type=F32, precision=lax.Precision.HIGHEST)


def _dot_nt(a, b):
    return lax.dot_general(a, b, (((1,), (1,)), ((), ())), preferred_element_type=F32)


def _mod_row(mod_ref, cnd, k):
    return mod_ref[pl.ds(cnd, 1), pl.ds(k * D, D)]


def _rms(x, g):
    return x * lax.rsqrt(jnp.mean(x * x, axis=-1, keepdims=True) + RMS_EPS) * g


def _mod_kernel(cond_ref, w_ref, b_ref, o_ref):
    c = cond_ref[...]
    s = c * jax.nn.sigmoid(c)
    o_ref[...] = _dot(s.astype(BF16), w_ref[...].astype(BF16)) + b_ref[...]


def _mod_call(cond8, ada_w, ada_b):
    tn = 1536
    return pl.pallas_call(
        _mod_kernel,
        grid=(DEPTH, N_MOD * D // tn),
        in_specs=[pl.BlockSpec((8, D), lambda l, j: (0, 0)),
                  pl.BlockSpec((None, D, tn), lambda l, j: (l, 0, j)),
                  pl.BlockSpec((None, 1, tn), lambda l, j: (l, 0, j))],
        out_specs=pl.BlockSpec((None, 8, tn), lambda l, j: (l, 0, j)),
        out_shape=jax.ShapeDtypeStruct((DEPTH, 8, N_MOD * D), F32),
        compiler_params=_cp(("parallel", "parallel")),
        name="adaln_mod",
    )(cond8, ada_w, ada_b.reshape(DEPTH, 1, N_MOD * D))


def _inproj_kernel(x_ref, mod_ref, g_ref, w_ref, hy_ref, fn_ref, q_ref, k_ref, v_ref, *,
                   cond_base, rows_per_cond, tm):
    cnd = cond_base + (pl.program_id(0) * tm) // rows_per_cond
    h = _rms(x_ref[...], g_ref[...]) * (1.0 + _mod_row(mod_ref, cnd, 1)) + _mod_row(mod_ref, cnd, 0)
    u = _dot(h.astype(BF16), w_ref[...])
    o1 = 3 * HY_W
    o2 = o1 + FN_W
    o3 = o2 + NA_W
    o4 = o3 + NA_W
    hy_ref[...] = u[:, :o1].astype(BF16)
    fn_ref[...] = u[:, o1:o2].astype(BF16)
    q_ref[...] = u[:, o2:o3].astype(BF16)
    k_ref[...] = u[:, o3:o4]
    v_ref[...] = u[:, o4:]


def _inproj_call(l, x, mod, norm_mix, w_in_b, cond_base, rows_per_cond):
    n = x.shape[0]
    tm = MERGE_TILE
    kern = functools.partial(_inproj_kernel, cond_base=cond_base, rows_per_cond=rows_per_cond, tm=tm)
    widths = (3 * HY_W, FN_W, NA_W, NA_W, NA_W)
    dts = (BF16, BF16, BF16, F32, F32)
    return pl.pallas_call(
        kern,
        grid=(n // tm,),
        in_specs=[pl.BlockSpec((tm, D), lambda i: (i, 0)),
                  pl.BlockSpec((None, 8, N_MOD * D), lambda i: (l, 0, 0)),
                  pl.BlockSpec((None, 1, D), lambda i: (l, 0, 0)),
                  pl.BlockSpec((None, D, BR_W), lambda i: (l, 0, 0))],
        out_specs=[pl.BlockSpec((tm, w), lambda i: (i, 0)) for w in widths],
        out_shape=[jax.ShapeDtypeStruct((n, w), dt) for w, dt in zip(widths, dts)],
        compiler_params=_cp(("parallel",)),
        name="in_proj",
    )(x, mod, norm_mix, w_in_b)


def _cs_tables(L, N):
    k = np.arange(L, dtype=np.int64)[:, None]
    na = np.arange(0, L, 64, dtype=np.int64)[None, :]
    nb = np.arange(64, dtype=np.int64)[None, :]
    ang_a = 2.0 * np.pi * ((k * na) % N).astype(np.float64) / N
    ang_b = 2.0 * np.pi * ((k * nb) % N).astype(np.float64) / N
    ca = jnp.asarray(np.cos(ang_a), F32)[:, :, None]
    sa = jnp.asarray(np.sin(ang_a), F32)[:, :, None]
    cb = jnp.asarray(np.cos(ang_b), F32)[:, None, :]
    sb = jnp.asarray(np.sin(ang_b), F32)[:, None, :]
    cos = (ca * cb - sa * sb).reshape(L, L)
    sin = (sa * cb + ca * sb).reshape(L, L)
    return cos, sin


def _hyena_tables(L):
    cos, sin = _cs_tables(L, 2 * L)
    alt = (1 - 2 * (jnp.arange(L) % 2)).astype(F32)
    row0 = (jnp.arange(L) == 0)
    s_fwd = jnp.where(row0[:, None], alt[None, :], sin)
    s_inv = jnp.where(row0[None, :], alt[:, None], sin)
    return cos.astype(BF16), s_fwd.astype(BF16), s_inv.astype(BF16)


def _fnet_tables(L):
    cos, sin = _cs_tables(L, L)
    g = FN_GROUP_DIM
    kk = np.arange(g)[:, None] * np.arange(g)[None, :]
    ang = 2.0 * np.pi * (kk % g) / g
    scale = 1.0 / math.sqrt(L * g)
    cb = np.kron(np.eye(FN_GROUPS), np.cos(ang)) * scale
    sb = np.kron(np.eye(FN_GROUPS), np.sin(ang)) * scale
    csb = jnp.asarray(np.concatenate([cb, sb], axis=1), F32).astype(BF16)
    return cos.astype(BF16), sin.astype(BF16), csb


def _hy_filt_a_kernel(z_ref, w1_ref, b1_ref, w2_ref, b2_ref, w3a_ref, w3b_ref, b3a_ref, b3b_ref,
                      fr_ref, dec_ref, fp_ref, fm_ref, last_ref, nyq_ref, pad_ref, *, L):
    z = z_ref[...]
    fr = fr_ref[...]
    a = jnp.sin(fr * (_dot_hi(z, w1_ref[...]) + b1_ref[...]))
    a = jnp.sin(fr * (_dot_hi(a, w2_ref[...]) + b2_ref[...]))
    window = jnp.exp(-z[:, 0:1] * jnp.abs(dec_ref[...]))
    f0 = (_dot_hi(a, w3a_ref[...]) + b3a_ref[...]) * window
    f1 = (_dot_hi(a, w3b_ref[...]) + b3b_ref[...]) * window
    norm = (jnp.sum(jnp.abs(f0), axis=0, keepdims=True)
            + jnp.sum(jnp.abs(f1), axis=0, keepdims=True) + 1e-6)
    f0 = f0 / norm
    f1 = f1 / norm
    pad_ref[pl.ds(0, 8), :] = jnp.zeros((8, 128), F32)
    pad_ref[pl.ds(8, L), :] = f1
    f1s = pad_ref[pl.ds(7, L), :]
    last = pad_ref[pl.ds(L + 7, 1), :]
    fp = f0 + f1s
    fp_ref[...] = fp.astype(BF16)
    fm_ref[...] = (f0 - f1s).astype(BF16)
    last_ref[...] = last
    row = lax.broadcasted_iota(I32, (L, 128), 0)
    alt = (1 - 2 * (row & 1)).astype(F32)
    nyq_ref[...] = jnp.sum(alt * fp, axis=0, keepdims=True) + last


def _hy_filt_b_kernel(c_ref, s_ref, fp_ref, fm_ref, last_ref, nyq_ref, hc_ref, hs_ref, *, tk):
    kt = pl.program_id(0)
    row = kt * tk + lax.broadcasted_iota(I32, (tk, HY_W), 0)
    alt = (1 - 2 * (row & 1)).astype(F32)
    hc_ref[...] = _dot(c_ref[...], fp_ref[...]) + alt * last_ref[...]
    hs = _dot(s_ref[...], fm_ref[...])
    hs_ref[...] = jnp.where(row == 0, nyq_ref[...], hs)


def _hyena_filters(L, tabs, p):
    cos, s_fwd, _ = tabs
    t = np.linspace(0.0, 1.0, L, dtype=np.float32)[:, None]
    bands = (HY_EMB - 1) // 2
    omega = (2.0 * math.pi * np.arange(L, dtype=np.float32)[:, None] / L).astype(np.float32)
    f = np.linspace(1e-4, bands - 1, bands, dtype=np.float32)[None, :]
    z = np.concatenate([t, np.cos(f * omega), -np.sin(f * omega)], axis=-1).astype(np.float32)
    z = jnp.asarray(np.pad(z, ((0, 0), (0, HY_EMB_PAD - HY_EMB))))
    w1 = jnp.pad(p["hy_ffn_w1"], ((0, 0), (0, HY_EMB_PAD - HY_EMB), (0, 0)))
    r3 = lambda a: a.reshape(DEPTH, 1, a.shape[-1])
    nch = HY_W // 128
    vec = lambda: pl.BlockSpec((None, 1, HY_FF), lambda l, c: (l, 0, 0))
    fp, fm, last, nyq = pl.pallas_call(
        functools.partial(_hy_filt_a_kernel, L=L),
        grid=(DEPTH, nch),
        in_specs=[pl.BlockSpec((L, HY_EMB_PAD), lambda l, c: (0, 0)),
                  pl.BlockSpec((None, HY_EMB_PAD, HY_FF), lambda l, c: (l, 0, 0)), vec(),
                  pl.BlockSpec((None, HY_FF, HY_FF), lambda l, c: (l, 0, 0)), vec(),
                  pl.BlockSpec((None, HY_FF, 128), lambda l, c: (l, 0, c)),
                  pl.BlockSpec((None, HY_FF, 128), lambda l, c: (l, 0, nch + c)),
                  pl.BlockSpec((None, 1, 128), lambda l, c: (l, 0, c)),
                  pl.BlockSpec((None, 1, 128), lambda l, c: (l, 0, nch + c)),
                  vec(),
                  pl.BlockSpec((None, 1, 128), lambda l, c: (l, 0, c))],
        out_specs=[pl.BlockSpec((None, L, 128), lambda l, c: (l, 0, c)),
                   pl.BlockSpec((None, L, 128), lambda l, c: (l, 0, c)),
                   pl.BlockSpec((None, 1, 128), lambda l, c: (l, 0, c)),
                   pl.BlockSpec((None, 1, 128), lambda l, c: (l, 0, c))],
        out_shape=[jax.ShapeDtypeStruct((DEPTH, L, HY_W), BF16),
                   jax.ShapeDtypeStruct((DEPTH, L, HY_W), BF16),
                   jax.ShapeDtypeStruct((DEPTH, 1, HY_W), F32),
                   jax.ShapeDtypeStruct((DEPTH, 1, HY_W), F32)],
        scratch_shapes=[pltpu.VMEM((L + 8, 128), F32)],
        compiler_params=_cp(("parallel", "parallel")),
        name="hyena_filter_taps",
    )(z, w1, r3(p["hy_ffn_b1"]), p["hy_ffn_w2"], r3(p["hy_ffn_b2"]), p["hy_ffn_w3"], p["hy_ffn_w3"],
      r3(p["hy_ffn_b3"]), r3(p["hy_ffn_b3"]), r3(p["hy_sin_freq"]), r3(p["hy_decay"]))
    tk = min(DFT_TILE, L)
    full = lambda: pl.BlockSpec((None, L, HY_W), lambda kt, l: (l, 0, 0))
    one = lambda: pl.BlockSpec((None, 1, HY_W), lambda kt, l: (l, 0, 0))
    hc, hs = pl.pallas_call(
        functools.partial(_hy_filt_b_kernel, tk=tk),
        grid=(L // tk, DEPTH),
        in_specs=[pl.BlockSpec((tk, L), lambda kt, l: (kt, 0)),
                  pl.BlockSpec((tk, L), lambda kt, l: (kt, 0)),
                  full(), full(), one(), one()],
        out_specs=[pl.BlockSpec((None, tk, HY_W), lambda kt, l: (l, kt, 0)),
                   pl.BlockSpec((None, tk, HY_W), lambda kt, l: (l, kt, 0))],
        out_shape=[jax.ShapeDtypeStruct((DEPTH, L, HY_W), F32)] * 2,
        compiler_params=_cp(("parallel", "parallel")),
        name="hyena_filter_dft",
    )(cos, s_fwd, fp, fm, last, nyq)
    return hc, hs


def _hy_pre_kernel(u0_ref, u1_ref, u2_ref, w0_ref, w1_ref, w2_ref, b0_ref, b1_ref, b2_ref,
                   zz_ref, x0_ref, pad_ref, *, L):
    zeros8 = jnp.zeros((8, 128), F32)
    pad_ref[pl.ds(0, 8), :] = zeros8
    pad_ref[pl.ds(L + 8, 8), :] = zeros8

    def conv(u_ref, w_ref, b_ref):
        pad_ref[pl.ds(8, L), :] = u_ref[...].astype(F32)
        w = w_ref[...]
        return (w[0:1] * pad_ref[pl.ds(7, L), :] + w[1:2] * pad_ref[pl.ds(8, L), :]
                + w[2:3] * pad_ref[pl.ds(9, L), :] + b_ref[...])

    x0_ref[...] = conv(u0_ref, w0_ref, b0_ref).astype(BF16)
    x1 = conv(u1_ref, w1_ref, b1_ref)
    v = conv(u2_ref, w2_ref, b2_ref)
    zz_ref[...] = (x1 * v).astype(BF16)


def _hy_fwd_kernel(c_ref, s_ref, zz_ref, hc_ref, hs_ref, pr_ref, pi_ref, *, tk, L):
    kt = pl.program_id(0)
    zz = zz_ref[...]
    zc = _dot(c_ref[...], zz)
    zs = _dot(s_ref[...], zz)
    hc = hc_ref[...]
    hs = hs_ref[...]
    row = kt * tk + lax.broadcasted_iota(I32, (tk, HY_W), 0)
    first = row == 0
    wgt = jnp.where(first, 1.0 / (2 * L), 2.0 / (2 * L)).astype(F32)
    ss = zs * hs
    pr_ref[...] = ((zc * hc - jnp.where(first, 0.0, ss)) * wgt).astype(BF16)
    pi_ref[...] = (jnp.where(first, ss, zc * hs + zs * hc) * wgt).astype(BF16)


def _hy_inv_kernel(c_ref, st_ref, pr_ref, pi_ref, zz_ref, x0_ref, skip_ref, o_ref):
    y = _dot(c_ref[...], pr_ref[...]) + _dot(st_ref[...], pi_ref[...])
    zz = zz_ref[...].astype(F32)
    o_ref[...] = (x0_ref[...].astype(F32) * (y + zz * skip_ref[...])).astype(BF16)


def _hyena_call(l, u_hy, B, L, tabs, hc, hs, p):
    cos, s_fwd, s_inv = tabs
    n = B * L
    nch = HY_W // 128
    cw = p["hy_conv_w"].reshape(DEPTH, 3, 3 * HY_W)
    cb = p["hy_conv_b"].reshape(DEPTH, 1, 3 * HY_W)
    ub = lambda part: pl.BlockSpec((L, 128), lambda b, c: (b, part * nch + c))
    wb = lambda part: pl.BlockSpec((None, 3, 128), lambda b, c: (l, 0, part * nch + c))
    bb = lambda part: pl.BlockSpec((None, 1, 128), lambda b, c: (l, 0, part * nch + c))
    zz, x0 = pl.pallas_call(
        functools.partial(_hy_pre_kernel, L=L),
        grid=(B, nch),
        in_specs=[ub(0), ub(1), ub(2), wb(0), wb(1), wb(2), bb(0), bb(1), bb(2)],
        out_specs=[pl.BlockSpec((L, 128), lambda b, c: (b, c))] * 2,
        out_shape=[jax.ShapeDtypeStruct((n, HY_W), BF16)] * 2,
        scratch_shapes=[pltpu.VMEM((L + 16, 128), F32)],
        compiler_params=_cp(("parallel", "parallel")),
        name="hyena_short_conv",
    )(u_hy, u_hy, u_hy, cw, cw, cw, cb, cb, cb)

    tk = min(DFT_TILE, L)
    nk = L // tk
    pr, pi = pl.pallas_call(
        functools.partial(_hy_fwd_kernel, tk=tk, L=L),
        grid=(nk, B),
        in_specs=[pl.BlockSpec((tk, L), lambda kt, b: (kt, 0)),
                  pl.BlockSpec((tk, L), lambda kt, b: (kt, 0)),
                  pl.BlockSpec((L, HY_W), lambda kt, b: (b, 0)),
                  pl.BlockSpec((None, tk, HY_W), lambda kt, b: (l, kt, 0)),
                  pl.BlockSpec((None, tk, HY_W), lambda kt, b: (l, kt, 0))],
        out_specs=[pl.BlockSpec((tk, HY_W), lambda kt, b: (b * nk + kt, 0))] * 2,
        out_shape=[jax.ShapeDtypeStruct((n, HY_W), BF16)] * 2,
        compiler_params=_cp(("parallel", "parallel")),
        name="hyena_dft_fwd",
    )(cos, s_fwd, zz, hc, hs)

    y = pl.pallas_call(
        _hy_inv_kernel,
        grid=(nk, B),
        in_specs=[pl.BlockSpec((tk, L), lambda qt, b: (qt, 0)),
                  pl.BlockSpec((tk, L), lambda qt, b: (qt, 0)),
                  pl.BlockSpec((L, HY_W), lambda qt, b: (b, 0)),
                  pl.BlockSpec((L, HY_W), lambda qt, b: (b, 0)),
                  pl.BlockSpec((tk, HY_W), lambda qt, b: (b * nk + qt, 0)),
                  pl.BlockSpec((tk, HY_W), lambda qt, b: (b * nk + qt, 0)),
                  pl.BlockSpec((None, 1, HY_W), lambda qt, b: (l, 0, 0))],
        out_specs=pl.BlockSpec((tk, HY_W), lambda qt, b: (b * nk + qt, 0)),
        out_shape=jax.ShapeDtypeStruct((n, HY_W), BF16),
        compiler_params=_cp(("parallel", "parallel")),
        name="hyena_dft_inv",
    )(cos, s_inv, pr, pi, zz, x0, p["hy_skip"].reshape(DEPTH, 1, HY_W))
    return y


def _fnet_kernel(u_ref, csb_ref, c_ref, s_ref, o_ref, uc_ref, us_ref):
    @pl.when(pl.program_id(1) == 0)
    def _():
        t = _dot(u_ref[...], csb_ref[...])
        uc_ref[...] = t[:, :FN_W].astype(BF16)
        us_ref[...] = t[:, FN_W:].astype(BF16)

    o_ref[...] = (_dot(c_ref[...], uc_ref[...]) - _dot(s_ref[...], us_ref[...])).astype(BF16)


def _fnet_call(u_fn, B, L, tabs):
    cos, sin, csb = tabs
    tq = min(DFT_TILE, L)
    nq = L // tq
    return pl.pallas_call(
        _fnet_kernel,
        grid=(B, nq),
        in_specs=[pl.BlockSpec((L, FN_W), lambda b, qt: (b, 0)),
                  pl.BlockSpec((FN_W, 2 * FN_W), lambda b, qt: (0, 0)),
                  pl.BlockSpec((tq, L), lambda b, qt: (qt, 0)),
                  pl.BlockSpec((tq, L), lambda b, qt: (qt, 0))],
        out_specs=pl.BlockSpec((tq, FN_W), lambda b, qt: (b * nq + qt, 0)),
        out_shape=jax.ShapeDtypeStruct((B * L, FN_W), BF16),
        scratch_shapes=[pltpu.VMEM((L, FN_W), BF16), pltpu.VMEM((L, FN_W), BF16)],
        compiler_params=_cp(("parallel", "arbitrary")),
        name="fnet_dft2",
    )(u_fn, csb, cos, sin)


def _ctx_attn_kernel(q_ref, k_ref, v_ref, o_ref):
    for h in range(NA_H):
        sl = slice(h * NA_D, (h + 1) * NA_D)
        q = q_ref[:, sl]
        k = k_ref[:, sl].astype(BF16)
        v = v_ref[:, sl].astype(BF16)
        s = _dot_nt(q, k) * ATTN_SCALE
        m = jnp.max(s, axis=-1, keepdims=True)
        e = jnp.exp(s - m)
        den = jnp.sum(e, axis=-1, keepdims=True)
        o_ref[:, sl] = (_dot(e.astype(BF16), v) / den).astype(BF16)


def _ctx_attn_call(q, k, v, B, L):
    blk = lambda: pl.BlockSpec((L, NA_W), lambda b: (b, 0))
    return pl.pallas_call(
        _ctx_attn_kernel,
        grid=(B,),
        in_specs=[blk(), blk(), blk()],
        out_specs=blk(),
        out_shape=jax.ShapeDtypeStruct((B * L, NA_W), BF16),
        compiler_params=_cp(("parallel",)),
        name="context_attention",
    )(q, k, v)


def _nbr_attn_kernel(q_ref, k_ref, v_ref, kc_ref, vc_ref, bias_ref, o_ref, *, rows, band_rows):
    r0 = pl.program_id(1) * NBR_GROUP
    bs = jnp.clip(r0 - WIN_R // 2, 0, rows - band_rows)
    start = pl.multiple_of(bs * GRID_W, GRID_W)
    band = band_rows * GRID_W
    for h in range(NA_H):
        sl = slice(h * NA_D, (h + 1) * NA_D)
        q = q_ref[:, sl]
        kb = k_ref[pl.ds(start, band), sl].astype(BF16)
        vb = v_ref[pl.ds(start, band), sl].astype(BF16)
        kc = kc_ref[:, sl].astype(BF16)
        vc = vc_ref[:, sl].astype(BF16)
        s_loc = _dot_nt(q, kb) * ATTN_SCALE + bias_ref[h]
        s_ctx = _dot_nt(q, kc) * ATTN_SCALE
        m = jnp.maximum(jnp.max(s_loc, axis=-1, keepdims=True), jnp.max(s_ctx, axis=-1, keepdims=True))
        e_loc = jnp.exp(s_loc - m)
        e_ctx = jnp.exp(s_ctx - m)
        den = jnp.sum(e_loc, axis=-1, keepdims=True) + jnp.sum(e_ctx, axis=-1, keepdims=True)
        o = _dot(e_loc.astype(BF16), vb) + _dot(e_ctx.astype(BF16), vc)
        o_ref[:, sl] = (o / den).astype(BF16)


def _nbr_bias_table(rel_bias, rows):
    wr = min(WIN_R, rows)
    band_rows = min(rows, wr + NBR_GROUP - 1)
    n_groups = rows // NBR_GROUP
    cols = np.arange(GRID_W)
    cs = np.clip(cols - WIN_C // 2, 0, GRID_W - WIN_C)
    mask = (cols[None, :] >= cs[:, None]) & (cols[None, :] < cs[:, None] + WIN_C)
    dc = np.clip(cols[None, :] - cols[:, None], -(WIN_C - 1), WIN_C - 1) + WIN_C - 1
    onehot = ((dc[None] == np.arange(2 * WIN_C - 1)[:, None, None]) & mask[None]).astype(np.float32)
    t = jnp.einsum("lhdj,jqk->lhdqk", rel_bias.astype(F32), jnp.asarray(onehot),
                   precision=lax.Precision.HIGHEST)
    t = t + jnp.asarray(np.where(mask, 0.0, MASK_NEG).astype(np.float32))
    masked = jnp.full(t.shape[:2] + t.shape[3:], MASK_NEG, F32)
    kinds = []
    for g in sorted({0, min(1, n_groups - 1), n_groups - 1}):
        r0 = g * NBR_GROUP
        bs = int(np.clip(r0 - wr // 2, 0, rows - band_rows))
        q_rows = []
        for i in range(NBR_GROUP):
            r = r0 + i
            rs = int(np.clip(r - wr // 2, 0, rows - wr))
            blocks = []
            for u in range(band_rows):
                key_row = bs + u
                blocks.append(t[:, :, key_row - r + WIN_R - 1] if rs <= key_row < rs + wr else masked)
            q_rows.append(jnp.concatenate(blocks, axis=-1))
        kinds.append(jnp.concatenate(q_rows, axis=-2))
    while len(kinds) < 3:
        kinds.append(kinds[-1])
    return jnp.stack(kinds, axis=1), band_rows


def _nbr_attn_call(l, q, k, v, ck, cv, bias, band_rows, B, L):
    rows = L // GRID_W
    n_groups = rows // NBR_GROUP
    P = ck.shape[2]
    gq = NBR_GROUP * GRID_W

    def kind(g):
        return jnp.where(g == 0, 0, jnp.where(g == n_groups - 1, 2, 1))

    return pl.pallas_call(
        functools.partial(_nbr_attn_kernel, rows=rows, band_rows=band_rows),
        grid=(B, n_groups),
        in_specs=[pl.BlockSpec((gq, NA_W), lambda b, g: (b * n_groups + g, 0)),
                  pl.BlockSpec((L, NA_W), lambda b, g: (b, 0)),
                  pl.BlockSpec((L, NA_W), lambda b, g: (b, 0)),
                  pl.BlockSpec((None, None, P, NA_W), lambda b, g: (b, l, 0, 0)),
                  pl.BlockSpec((None, None, P, NA_W), lambda b, g: (b, l, 0, 0)),
                  pl.BlockSpec((None, None, NA_H, gq, band_rows * GRID_W),
                               lambda b, g: (l, kind(g), 0, 0, 0))],
        out_specs=pl.BlockSpec((gq, NA_W), lambda b, g: (b * n_groups + g, 0)),
        out_shape=jax.ShapeDtypeStruct((B * L, NA_W), BF16),
        compiler_params=_cp(("parallel", "parallel")),
        name="neighbourhood_attention",
    )(q, k, v, ck, cv, bias)


def _merge_kernel(x_ref, yhy_ref, yfn_ref, yna_ref, mod_ref, gmix_ref, gffn_ref, wg_ref, why_ref,
                  wfn_ref, wna_ref, wout_ref, rw_ref, rb_ref, cnt0_ref,
                  xn_ref, h2_ref, idx_ref, gate_ref, rank_ref, cnt_ref, run_ref, *,
                  cond_base, rows_per_cond, tm):
    i = pl.program_id(0)
    cnd = cond_base + (i * tm) // rows_per_cond

    @pl.when(i == 0)
    def _():
        run_ref[...] = cnt0_ref[...]

    x = x_ref[...]
    h = (_rms(x, gmix_ref[...]) * (1.0 + _mod_row(mod_ref, cnd, 1)) + _mod_row(mod_ref, cnd, 0)).astype(BF16)
    merged = None
    for b, (y_ref, w_ref) in enumerate(((yhy_ref, why_ref), (yfn_ref, wfn_ref), (yna_ref, wna_ref))):
        gate_b = jax.nn.sigmoid(_dot(h, wg_ref[:, b * D:(b + 1) * D]))
        term = gate_b * _dot(y_ref[...], w_ref[...])
        merged = term if merged is None else merged + term
    xn = x + _mod_row(mod_ref, cnd, 2) * _dot(merged.astype(BF16), wout_ref[...])
    xn_ref[...] = xn
    h2 = _rms(xn, gffn_ref[...]) * (1.0 + _mod_row(mod_ref, cnd, 4)) + _mod_row(mod_ref, cnd, 3)
    for cc in range(D // 128):
        h2_ref[pl.ds(cc, tm, stride=8), :] = h2[:, cc * 128:(cc + 1) * 128]

    h2_hi = h2.astype(BF16)
    h2_lo = (h2 - h2_hi.astype(F32)).astype(BF16)
    rw = rw_ref[...]
    rw_hi = rw.astype(BF16)
    rw_lo = (rw - rw_hi.astype(F32)).astype(BF16)
    logits = _dot(h2_hi, rw_hi) + _dot(h2_lo, rw_hi) + _dot(h2_hi, rw_lo) + rb_ref[...]
    lane = lax.broadcasted_iota(I32, (tm, N_EXP), 1)
    lane4 = lax.broadcasted_iota(I32, (tm, TOP_K), 1)
    rem = logits
    vals, idxs = [], []
    for _ in range(TOP_K):
        m = jnp.max(rem, axis=-1, keepdims=True)
        ix = jnp.min(jnp.where(rem == m, lane, N_EXP), axis=-1, keepdims=True)
        vals.append(m)
        idxs.append(ix)
        rem = jnp.where(lane == ix, -jnp.inf, rem)
    es = [jnp.exp(v - vals[0]) for v in vals]
    den = es[0] + es[1] + es[2] + es[3]

    onehot = jnp.zeros((tm, N_EXP), F32)
    for ix in idxs:
        onehot = onehot + (lane == ix).astype(F32)
    r_i = lax.broadcasted_iota(I32, (tm, tm), 0)
    c_i = lax.broadcasted_iota(I32, (tm, tm), 1)
    lower = (c_i < r_i).astype(BF16)
    before = _dot(lower, onehot.astype(BF16)) + run_ref[...]

    idx_o = jnp.zeros((tm, TOP_K), I32)
    gate_o = jnp.zeros((tm, TOP_K), F32)
    rank_o = jnp.zeros((tm, TOP_K), F32)
    for kk in range(TOP_K):
        rk = jnp.sum(jnp.where(lane == idxs[kk], before, 0.0), axis=-1, keepdims=True)
        idx_o = jnp.where(lane4 == kk, idxs[kk], idx_o)
        gate_o = jnp.where(lane4 == kk, es[kk] / den, gate_o)
        rank_o = jnp.where(lane4 == kk, rk, rank_o)
    idx_ref[...] = idx_o
    gate_ref[...] = gate_o
    rank_ref[...] = rank_o.astype(I32)
    run_ref[...] = run_ref[...] + jnp.sum(onehot, axis=0, keepdims=True)
    cnt_ref[...] = run_ref[...]


def _merge_call(l, x, y_hy, y_fn, y_na, mod, wts, cnt0, cond_base, rows_per_cond):
    n = x.shape[0]
    tm = MERGE_TILE
    kern = functools.partial(_merge_kernel, cond_base=cond_base, rows_per_cond=rows_per_cond, tm=tm)
    row = lambda w: pl.BlockSpec((tm, w), lambda i: (i, 0))
    lw = lambda a, b: pl.BlockSpec((None, a, b), lambda i: (l, 0, 0), pipeline_mode=pl.Buffered(1))
    return pl.pallas_call(
        kern,
        grid=(n // tm,),
        in_specs=[row(D), row(HY_W), row(FN_W), row(NA_W),
                  lw(8, N_MOD * D), lw(1, D), lw(1, D),
                  lw(D, 3 * D), lw(HY_W, D), lw(FN_W, D), lw(NA_W, D), lw(D, D),
                  lw(D, N_EXP), lw(1, N_EXP),
                  pl.BlockSpec((1, N_EXP), lambda i: (0, 0))],
        out_specs=[row(D), pl.BlockSpec((tm * 8, 128), lambda i: (i, 0)), row(TOP_K), row(TOP_K), row(TOP_K),
                   pl.BlockSpec((1, N_EXP), lambda i: (0, 0))],
        out_shape=[jax.ShapeDtypeStruct((n, D), F32), jax.ShapeDtypeStruct((n * 8, 128), F32),
                   jax.ShapeDtypeStruct((n, TOP_K), I32), jax.ShapeDtypeStruct((n, TOP_K), F32),
                   jax.ShapeDtypeStruct((n, TOP_K), I32), jax.ShapeDtypeStruct((1, N_EXP), F32)],
        scratch_shapes=[pltpu.VMEM((1, N_EXP), F32)],
        compiler_params=_cp(("arbitrary",)),
        name="merge_router",
    )(x, y_hy, y_fn, y_na, mod, wts["norm_mix"], wts["norm_ffn"], wts["w_g"], wts["w_hy_out"],
      wts["w_fn_out"], wts["w_na_out"], wts["w_out"], wts["router_w"], wts["router_b"], cnt0)


def _dispatch_kernel(slot_ref, h_ref, xs_in_ref, xs_ref, sem, *, tm):
    del xs_in_ref

    def issue(t, carry):
        src = h_ref.at[pl.ds(pl.multiple_of(t * 8, 8), 8)]
        for kk in range(TOP_K):
            s = slot_ref[0, 0, t * TOP_K + kk]
            dst = xs_ref.at[pl.ds(pl.multiple_of(s * 8, 8), 8)]
            pltpu.make_async_copy(src, dst, sem).start(priority=kk % 2)
        return carry

    lax.fori_loop(0, tm, issue, 0)
    for kk in range(TOP_K):
        pltpu.make_async_copy(h_ref, xs_ref.at[pl.ds(0, tm * 8)], sem).wait()


def _dispatch_call(slot, h2, xs):
    n = h2.shape[0] // 8
    tm = ROW_TILE
    slot3 = slot.reshape(n // tm, 1, tm * TOP_K)
    return pl.pallas_call(
        functools.partial(_dispatch_kernel, tm=tm),
        grid=(n // tm,),
        in_specs=[pl.BlockSpec((1, 1, tm * TOP_K), lambda i: (i, 0, 0), memory_space=pltpu.SMEM),
                  pl.BlockSpec((tm * 8, 128), lambda i: (i, 0)),
                  pl.BlockSpec(memory_space=pl.ANY)],
        out_specs=pl.BlockSpec(memory_space=pl.ANY),
        out_shape=jax.ShapeDtypeStruct(xs.shape, xs.dtype),
        scratch_shapes=[pltpu.SemaphoreType.DMA(())],
        input_output_aliases={2: 0},
        compiler_params=_cp(("arbitrary",)),
        name="moe_dispatch",
    )(slot3, h2, xs)


def _moe_kernel(be_ref, nu_ref, x_ref, w1_ref, b1g_ref, b1l_ref, w2_ref, b2_ref, y_ref,
                w1g_s, w1l_s, w2_s):
    j = pl.program_id(0)
    live = j < nu_ref[0]
    new_expert = jnp.logical_or(j == 0, be_ref[j] != be_ref[jnp.maximum(j - 1, 0)])

    @pl.when(jnp.logical_and(live, new_expert))
    def _():
        grp = 2 * DEINT
        r = lax.broadcasted_iota(I32, (grp, grp), 0)
        c = lax.broadcasted_iota(I32, (grp, grp), 1)
        perm = (r == jnp.where(c < DEINT, 2 * c, 2 * (c - DEINT) + 1)).astype(BF16)
        for ch in range(2 * FF // grp):
            t = _dot(w1_ref[:, ch * grp:(ch + 1) * grp].astype(BF16), perm).astype(BF16)
            w1g_s[:, ch * DEINT:(ch + 1) * DEINT] = t[:, :DEINT]
            w1l_s[:, ch * DEINT:(ch + 1) * DEINT] = t[:, DEINT:]
        w2_s[...] = w2_ref[...].astype(BF16)

    @pl.when(live)
    def _():
        x = jnp.concatenate([x_ref[pl.ds(cc, MOE_ROWS, stride=8), :] for cc in range(D // 128)],
                            axis=1).astype(BF16)
        g = jnp.minimum(_dot(x, w1g_s[...]) + b1g_ref[...], SWIGLU_LIMIT)
        lin = jnp.clip(_dot(x, w1l_s[...]) + b1l_ref[...], -SWIGLU_LIMIT, SWIGLU_LIMIT)
        act = g * jax.nn.sigmoid(SWIGLU_ALPHA * g) * (lin + 1.0)
        y = _dot(act.astype(BF16), w2_s[...]) + b2_ref[...]
        for cc in range(D // 128):
            y_ref[pl.ds(cc, MOE_ROWS, stride=8), :] = y[:, cc * 128:(cc + 1) * 128]

    @pl.when(jnp.logical_not(live))
    def _():
        y_ref[...] = jnp.zeros_like(y_ref)


def _moe_call(l, xs, block_expert, n_used, wts):
    n_slots = xs.shape[0] // 8
    R = MOE_ROWS
    bspec = lambda: pl.BlockSpec((None, None, 1, FF), lambda j, be, nu: (l, be[j], 0, 0))
    grid_spec = pltpu.PrefetchScalarGridSpec(
        num_scalar_prefetch=2,
        grid=(n_slots // R,),
        in_specs=[pl.BlockSpec((R * 8, 128), lambda j, be, nu: (jnp.minimum(j, nu[0] - 1), 0)),
                  pl.BlockSpec((None, None, D, 2 * FF), lambda j, be, nu: (l, be[j], 0, 0)),
                  bspec(), bspec(),
                  pl.BlockSpec((None, None, FF, D), lambda j, be, nu: (l, be[j], 0, 0)),
                  bspec()],
        out_specs=pl.BlockSpec((R * 8, 128), lambda j, be, nu: (j, 0)),
        scratch_shapes=[pltpu.VMEM((D, FF), BF16), pltpu.VMEM((D, FF), BF16),
                        pltpu.VMEM((FF, D), BF16)],
    )
    return pl.pallas_call(
        _moe_kernel,
        grid_spec=grid_spec,
        out_shape=jax.ShapeDtypeStruct((n_slots * 8, 128), F32),
        compiler_params=_cp(("arbitrary",)),
        name="moe_experts",
    )(block_expert, n_used, xs, wts["w1"], wts["b1g"], wts["b1l"], wts["w2"], wts["b2"])


def _combine_kernel(slot_ref, xn_ref, gate_ref, mod_ref, fin_ref, ys_ref, o_ref, ybuf, sem, *,
                    cond_base, rows_per_cond, tm, final):
    cnd = cond_base + (pl.program_id(0) * tm) // rows_per_cond

    def issue(t, carry):
        for kk in range(TOP_K):
            s = slot_ref[0, 0, t * TOP_K + kk]
            src = ys_ref.at[pl.ds(pl.multiple_of(s * 8, 8), 8)]
            dst = ybuf.at[kk, pl.ds(pl.multiple_of(t * 8, 8), 8)]
            pltpu.make_async_copy(src, dst, sem).start(priority=kk % 2)
        return carry

    lax.fori_loop(0, tm, issue, 0)
    for kk in range(TOP_K):
        pltpu.make_async_copy(ys_ref.at[pl.ds(0, tm * 8)], ybuf.at[kk], sem).wait()
    gate = gate_ref[...]
    chunks = []
    for cc in range(D // 128):
        part = gate[:, 0:1] * ybuf[0, pl.ds(cc, tm, stride=8), :]
        for kk in range(1, TOP_K):
            part = part + gate[:, kk:kk + 1] * ybuf[kk, pl.ds(cc, tm, stride=8), :]
        chunks.append(part)
    acc = jnp.concatenate(chunks, axis=1)
    out = xn_ref[...] + _mod_row(mod_ref, cnd, 5) * acc
    if final:
        out = _rms(out, fin_ref[...])
    o_ref[...] = out


def _combine_call(l, slot, xn, gate, mod, final_norm, ys, cond_base, rows_per_cond, final):
    n = xn.shape[0]
    tm = ROW_TILE
    slot3 = slot.reshape(n // tm, 1, tm * TOP_K)
    kern = functools.partial(_combine_kernel, cond_base=cond_base, rows_per_cond=rows_per_cond,
                             tm=tm, final=final)
    return pl.pallas_call(
        kern,
        grid=(n // tm,),
        in_specs=[pl.BlockSpec((1, 1, tm * TOP_K), lambda i: (i, 0, 0), memory_space=pltpu.SMEM),
                  pl.BlockSpec((tm, D), lambda i: (i, 0)),
                  pl.BlockSpec((tm, TOP_K), lambda i: (i, 0)),
                  pl.BlockSpec((None, 8, N_MOD * D), lambda i: (l, 0, 0)),
                  pl.BlockSpec((1, D), lambda i: (0, 0)),
                  pl.BlockSpec(memory_space=pl.ANY)],
        out_specs=pl.BlockSpec((tm, D), lambda i: (i, 0)),
        out_shape=jax.ShapeDtypeStruct((n, D), F32),
        scratch_shapes=[pltpu.VMEM((TOP_K, tm * 8, 128), F32), pltpu.SemaphoreType.DMA(())],
        compiler_params=_cp(("arbitrary",)),
        name="moe_combine",
    )(slot3, xn, gate, mod, final_norm.reshape(1, D), ys)


def kernel(x_prompt, x_sample, cache_k, cache_v, c, c_ctx, ada_w, ada_b, norm_mix, norm_ffn, w_in,
           hy_conv_w, hy_conv_b, hy_ffn_w1, hy_ffn_b1, hy_ffn_w2, hy_ffn_b2, hy_ffn_w3, hy_ffn_b3,
           hy_sin_freq, hy_decay, hy_skip, w_hy_out, w_fn_out, w_na_out, na_rel_bias, w_out,
           router_w, router_b, moe_w1, moe_b1, moe_w2, moe_b2, final_norm):
    Bp, Lp, _ = x_prompt.shape
    Bs, Ls, _ = x_sample.shape
    P = cache_k.shape[2]
    assert Bs + 1 <= 8 and Lp % ROW_TILE == 0 and Ls % MERGE_TILE == 0 and Ls % GRID_W == 0
    assert (Bp * Lp) % MERGE_TILE == 0
    n_p, n_s = Bp * Lp, Bs * Ls
    hyp = dict(hy_conv_w=hy_conv_w, hy_conv_b=hy_conv_b, hy_ffn_w1=hy_ffn_w1, hy_ffn_b1=hy_ffn_b1,
               hy_ffn_w2=hy_ffn_w2, hy_ffn_b2=hy_ffn_b2, hy_ffn_w3=hy_ffn_w3, hy_ffn_b3=hy_ffn_b3,
               hy_sin_freq=hy_sin_freq, hy_decay=hy_decay, hy_skip=hy_skip)

    r3 = lambda a: a.reshape(DEPTH, 1, a.shape[-1])
    wts = dict(
        norm_mix=r3(norm_mix), norm_ffn=r3(norm_ffn),
        w_g=w_in[:, :, BR_W:].astype(BF16), w_hy_out=w_hy_out.astype(BF16),
        w_fn_out=w_fn_out.astype(BF16), w_na_out=w_na_out.astype(BF16), w_out=w_out.astype(BF16),
        router_w=router_w, router_b=r3(router_b),
        w1=moe_w1,
        b1g=moe_b1[..., 0::2].reshape(DEPTH, N_EXP, 1, FF),
        b1l=moe_b1[..., 1::2].reshape(DEPTH, N_EXP, 1, FF),
        w2=moe_w2, b2=moe_b2.reshape(DEPTH, N_EXP, 1, D),
    )
    w_in_b = w_in[:, :, :BR_W].astype(BF16)

    cond8 = jnp.zeros((8, D), F32).at[0].set(c_ctx).at[1:1 + Bs].set(c)
    mod = _mod_call(cond8, ada_w, ada_b)

    streams = []
    for (B, L, base, rpc) in ((Bp, Lp, 0, Bp * Lp), (Bs, Ls, 1, Ls)):
        hy_tabs = _hyena_tables(L)
        hc, hs = _hyena_filters(L, hy_tabs, hyp)
        streams.append(dict(B=B, L=L, base=base, rpc=rpc, hy_tabs=hy_tabs, hc=hc, hs=hs,
                            fn_tabs=_fnet_tables(L)))
    rows_s = Ls // GRID_W
    assert rows_s % NBR_GROUP == 0 and rows_s >= WIN_R + NBR_GROUP - 1
    bias_tab, band_rows = _nbr_bias_table(na_rel_bias, rows_s)
    ck = cache_k.reshape(Bs, DEPTH, P, NA_W)
    cv = cache_v.reshape(Bs, DEPTH, P, NA_W)

    n_assign = (n_p + n_s) * TOP_K
    n_blocks = -(-n_assign // MOE_ROWS) + N_EXP
    n_slots = n_blocks * MOE_ROWS

    xs_tok = [x_prompt.reshape(n_p, D), x_sample.reshape(n_s, D)]
    new_k, new_v = [], []
    for l in range(DEPTH):
        merged = []
        cnt = jnp.zeros((1, N_EXP), F32)
        for si, st in enumerate(streams):
            B, L = st["B"], st["L"]
            x = xs_tok[si]
            u_hy, u_fn, q, k, v = _inproj_call(l, x, mod, wts["norm_mix"], w_in_b, st["base"], st["rpc"])
            y_hy = _hyena_call(l, u_hy, B, L, st["hy_tabs"], st["hc"], st["hs"], hyp)
            y_fn = _fnet_call(u_fn, B, L, st["fn_tabs"])
            if si == 0:
                y_na = _ctx_attn_call(q, k, v, B, L)
                new_k.append(k.reshape(B, L, NA_H, NA_D))
                new_v.append(v.reshape(B, L, NA_H, NA_D))
            else:
                y_na = _nbr_attn_call(l, q, k, v, ck, cv, bias_tab, band_rows, B, L)
            xn, h2, idx, gate, rank, cnt = _merge_call(l, x, y_hy, y_fn, y_na, mod, wts, cnt,
                                                       st["base"], st["rpc"])
            merged.append((xn, h2, idx, gate, rank))

        counts = cnt[0].astype(I32)
        blocks_per_e = (counts + MOE_ROWS - 1) // MOE_ROWS
        block_end = jnp.cumsum(blocks_per_e)
        block_start = block_end - blocks_per_e
        block_expert = jnp.minimum(
            jnp.sum(jnp.arange(n_blocks, dtype=I32)[:, None] >= block_end[None, :], axis=1),
            N_EXP - 1).astype(I32)
        n_used = block_end[-1:].astype(I32)

        xs = jnp.zeros((n_slots * 8, 128), F32)
        slots = []
        for (xn, h2, idx, gate, rank) in merged:
            start_of = jnp.sum(jnp.where(idx[..., None] == jnp.arange(N_EXP, dtype=I32), block_start, 0), axis=-1)
            slot = start_of * MOE_ROWS + rank
            slots.append(slot)
            xs = _dispatch_call(slot, h2, xs)
        ys = _moe_call(l, xs, block_expert, n_used, wts)
        for si, st in enumerate(streams):
            xn, h2, idx, gate, rank = merged[si]
            xs_tok[si] = _combine_call(l, slots[si], xn, gate, mod, final_norm, ys,
                                       st["base"], st["rpc"], l == DEPTH - 1)

    y_prompt = xs_tok[0].reshape(Bp, Lp, D)
    y_sample = xs_tok[1].reshape(Bs, Ls, D)
    return (y_prompt, y_sample, jnp.stack(new_k, axis=1), jnp.stack(new_v, axis=1))
```

```python
import functools
import math

import numpy as np
import jax
import jax.numpy as jnp
from jax import lax
from jax.experimental import pallas as pl
from jax.experimental.pallas import tpu as pltpu

F32 = jnp.float32
BF16 = jnp.bfloat16
I32 = jnp.int32

D = 1024
DEPTH = 4
N_MOD = 6
RMS_EPS = 1e-6
HY_W = 384
HY_EMB = 33
HY_EMB_PAD = 128
HY_FF = 64
FN_W = 256
FN_GROUPS = 4
FN_GROUP_DIM = 64
NA_H = 6
NA_D = 64
NA_W = NA_H * NA_D
GRID_W = 64
WIN_R = 8
WIN_C = 16
NBR_GROUP = 4
ATTN_SCALE = NA_D ** -0.5
BR_W = 3 * HY_W + FN_W + 3 * NA_W
N_EXP = 32
TOP_K = 4
FF = 1024
SWIGLU_LIMIT = 7.0
SWIGLU_ALPHA = 1.702
MASK_NEG = -1e30
DEINT = 128

ROW_TILE = 256
MERGE_TILE = 512
MOE_ROWS = 512
DFT_TILE = 512
VMEM_LIMIT = 56 * 1024 * 1024


def _cp(sem, vmem=VMEM_LIMIT):
    return pltpu.CompilerParams(dimension_semantics=sem, vmem_limit_bytes=vmem)


def _dot(a, b):
    return jnp.dot(a, b, preferred_element_type=F32)


def _dot_hi(a, b):
    return jnp.dot(a, b, preferred_element_type=F32, precision=lax.Precision.HIGHEST)


def _dot_nt(a, b):
    return lax.dot_general(a, b, (((1,), (1,)), ((), ())), preferred_element_type=F32)


def _mod_row(mod_ref, cnd, k):
    return mod_ref[pl.ds(cnd, 1), pl.ds(k * D, D)]


def _rms(x, g):
    return x * lax.rsqrt(jnp.mean(x * x, axis=-1, keepdims=True) + RMS_EPS) * g


def _mod_kernel(cond_ref, w_ref, b_ref, o_ref):
    c = cond_ref[...]
    s = c * jax.nn.sigmoid(c)
    o_ref[...] = _dot(s.astype(BF16), w_ref[...].astype(BF16)) + b_ref[...]


def _mod_call(cond8, ada_w, ada_b):
    tn = 1536
    return pl.pallas_call(
        _mod_kernel,
        grid=(DEPTH, N_MOD * D // tn),
        in_specs=[pl.BlockSpec((8, D), lambda l, j: (0, 0)),
                  pl.BlockSpec((None, D, tn), lambda l, j: (l, 0, j)),
                  pl.BlockSpec((None, 1, tn), lambda l, j: (l, 0, j))],
        out_specs=pl.BlockSpec((None, 8, tn), lambda l, j: (l, 0, j)),
        out_shape=jax.ShapeDtypeStruct((DEPTH, 8, N_MOD * D), F32),
        compiler_params=_cp(("parallel", "parallel")),
        name="adaln_mod",
    )(cond8, ada_w, ada_b.reshape(DEPTH, 1, N_MOD * D))


def _inproj_kernel(x_ref, mod_ref, g_ref, w_ref, hy_ref, fn_ref, q_ref, k_ref, v_ref, *,
                   cond_base, rows_per_cond, tm):
    cnd = cond_base + (pl.program_id(0) * tm) // rows_per_cond
    h = _rms(x_ref[...], g_ref[...]) * (1.0 + _mod_row(mod_ref, cnd, 1)) + _mod_row(mod_ref, cnd, 0)
    u = _dot(h.astype(BF16), w_ref[...])
    o1 = 3 * HY_W
    o2 = o1 + FN_W
    o3 = o2 + NA_W
    o4 = o3 + NA_W
    hy_ref[...] = u[:, :o1].astype(BF16)
    fn_ref[...] = u[:, o1:o2].astype(BF16)
    q_ref[...] = u[:, o2:o3].astype(BF16)
    k_ref[...] = u[:, o3:o4]
    v_ref[...] = u[:, o4:]


def _inproj_call(l, x, mod, norm_mix, w_in_b, cond_base, rows_per_cond):
    n = x.shape[0]
    tm = MERGE_TILE
    kern = functools.partial(_inproj_kernel, cond_base=cond_base, rows_per_cond=rows_per_cond, tm=tm)
    widths = (3 * HY_W, FN_W, NA_W, NA_W, NA_W)
    dts = (BF16, BF16, BF16, F32, F32)
    return pl.pallas_call(
        kern,
        grid=(n // tm,),
        in_specs=[pl.BlockSpec((tm, D), lambda i: (i, 0)),
                  pl.BlockSpec((None, 8, N_MOD * D), lambda i: (l, 0, 0)),
                  pl.BlockSpec((None, 1, D), lambda i: (l, 0, 0)),
                  pl.BlockSpec((None, D, BR_W), lambda i: (l, 0, 0))],
        out_specs=[pl.BlockSpec((tm, w), lambda i: (i, 0)) for w in widths],
        out_shape=[jax.ShapeDtypeStruct((n, w), dt) for w, dt in zip(widths, dts)],
        compiler_params=_cp(("parallel",)),
        name="in_proj",
    )(x, mod, norm_mix, w_in_b)


def _cs_tables(L, N):
    k = np.arange(L, dtype=np.int64)[:, None]
    na = np.arange(0, L, 64, dtype=np.int64)[None, :]
    nb = np.arange(64, dtype=np.int64)[None, :]
    ang_a = 2.0 * np.pi * ((k * na) % N).astype(np.float64) / N
    ang_b = 2.0 * np.pi * ((k * nb) % N).astype(np.float64) / N
    ca = jnp.asarray(np.cos(ang_a), F32)[:, :, None]
    sa = jnp.asarray(np.sin(ang_a), F32)[:, :, None]
    cb = jnp.asarray(np.cos(ang_b), F32)[:, None, :]
    sb = jnp.asarray(np.sin(ang_b), F32)[:, None, :]
    cos = (ca * cb - sa * sb).reshape(L, L)
    sin = (sa * cb + ca * sb).reshape(L, L)
    return cos, sin


def _hyena_tables(L):
    cos, sin = _cs_tables(L, 2 * L)
    alt = (1 - 2 * (jnp.arange(L) % 2)).astype(F32)
    row0 = (jnp.arange(L) == 0)
    s_fwd = jnp.where(row0[:, None], alt[None, :], sin)
    s_inv = jnp.where(row0[None, :], alt[:, None], sin)
    return cos.astype(BF16), s_fwd.astype(BF16), s_inv.astype(BF16)


def _fnet_tables(L):
    cos, sin = _cs_tables(L, L)
    g = FN_GROUP_DIM
    kk = np.arange(g)[:, None] * np.arange(g)[None, :]
    ang = 2.0 * np.pi * (kk % g) / g
    scale = 1.0 / math.sqrt(L * g)
    cb = np.kron(np.eye(FN_GROUPS), np.cos(ang)) * scale
    sb = np.kron(np.eye(FN_GROUPS), np.sin(ang)) * scale
    csb = jnp.asarray(np.concatenate([cb, sb], axis=1), F32).astype(BF16)
    return cos.astype(BF16), sin.astype(BF16), csb


def _hy_filt_a_kernel(z_ref, w1_ref, b1_ref, w2_ref, b2_ref, w3a_ref, w3b_ref, b3a_ref, b3b_ref,
                      fr_ref, dec_ref, fp_ref, fm_ref, last_ref, nyq_ref, pad_ref, a_ref, *, L):
    z = z_ref[...]

    @pl.when(pl.program_id(1) == 0)
    def _():
        fr = fr_ref[...]
        a1 = jnp.sin(fr * (_dot_hi(z, w1_ref[...]) + b1_ref[...]))
        a_ref[...] = jnp.sin(fr * (_dot_hi(a1, w2_ref[...]) + b2_ref[...]))

    a = a_ref[...]
    window = jnp.exp(-z[:, 0:1] * jnp.abs(dec_ref[...]))
    f0 = (_dot_hi(a, w3a_ref[...]) + b3a_ref[...]) * window
    f1 = (_dot_hi(a, w3b_ref[...]) + b3b_ref[...]) * window
    norm = (jnp.sum(jnp.abs(f0), axis=0, keepdims=True)
            + jnp.sum(jnp.abs(f1), axis=0, keepdims=True) + 1e-6)
    f0 = f0 / norm
    f1 = f1 / norm
    pad_ref[pl.ds(0, 8), :] = jnp.zeros((8, 128), F32)
    pad_ref[pl.ds(8, L), :] = f1
    f1s = pad_ref[pl.ds(7, L), :]
    last = pad_ref[pl.ds(L + 7, 1), :]
    fp = f0 + f1s
    fp_ref[...] = fp.astype(BF16)
    fm_ref[...] = (f0 - f1s).astype(BF16)
    last_ref[...] = last
    row = lax.broadcasted_iota(I32, (L, 128), 0)
    alt = (1 - 2 * (row & 1)).astype(F32)
    nyq_ref[...] = jnp.sum(alt * fp, axis=0, keepdims=True) + last


def _hy_filt_b_kernel(c_ref, s_ref, fp_ref, fm_ref, last_ref, nyq_ref, hc_ref, hs_ref, *, tk):
    kt = pl.program_id(0)
    row = kt * tk + lax.broadcasted_iota(I32, (tk, HY_W), 0)
    alt = (1 - 2 * (row & 1)).astype(F32)
    hc_ref[...] = _dot(c_ref[...], fp_ref[...]) + alt * last_ref[...]
    hs = _dot(s_ref[...], fm_ref[...])
    hs_ref[...] = jnp.where(row == 0, nyq_ref[...], hs)


def _hyena_filters(L, tabs, p):
    cos, s_fwd, _ = tabs
    t = np.linspace(0.0, 1.0, L, dtype=np.float32)[:, None]
    bands = (HY_EMB - 1) // 2
    omega = (2.0 * math.pi * np.arange(L, dtype=np.float32)[:, None] / L).astype(np.float32)
    f = np.linspace(1e-4, bands - 1, bands, dtype=np.float32)[None, :]
    z = np.concatenate([t, np.cos(f * omega), -np.sin(f * omega)], axis=-1).astype(np.float32)
    z = jnp.asarray(np.pad(z, ((0, 0), (0, HY_EMB_PAD - HY_EMB))))
    w1 = jnp.pad(p["hy_ffn_w1"], ((0, 0), (0, HY_EMB_PAD - HY_EMB), (0, 0)))
    r3 = lambda a: a.reshape(DEPTH, 1, a.shape[-1])
    nch = HY_W // 128
    vec = lambda: pl.BlockSpec((None, 1, HY_FF), lambda l, c: (l, 0, 0))
    fp, fm, last, nyq = pl.pallas_call(
        functools.partial(_hy_filt_a_kernel, L=L),
        grid=(DEPTH, nch),
        in_specs=[pl.BlockSpec((L, HY_EMB_PAD), lambda l, c: (0, 0)),
                  pl.BlockSpec((None, HY_EMB_PAD, HY_FF), lambda l, c: (l, 0, 0)), vec(),
                  pl.BlockSpec((None, HY_FF, HY_FF), lambda l, c: (l, 0, 0)), vec(),
                  pl.BlockSpec((None, HY_FF, 128), lambda l, c: (l, 0, c)),
                  pl.BlockSpec((None, HY_FF, 128), lambda l, c: (l, 0, nch + c)),
                  pl.BlockSpec((None, 1, 128), lambda l, c: (l, 0, c)),
                  pl.BlockSpec((None, 1, 128), lambda l, c: (l, 0, nch + c)),
                  vec(),
                  pl.BlockSpec((None, 1, 128), lambda l, c: (l, 0, c))],
        out_specs=[pl.BlockSpec((None, L, 128), lambda l, c: (l, 0, c)),
                   pl.BlockSpec((None, L, 128), lambda l, c: (l, 0, c)),
                   pl.BlockSpec((None, 1, 128), lambda l, c: (l, 0, c)),
                   pl.BlockSpec((None, 1, 128), lambda l, c: (l, 0, c))],
        out_shape=[jax.ShapeDtypeStruct((DEPTH, L, HY_W), BF16),
                   jax.ShapeDtypeStruct((DEPTH, L, HY_W), BF16),
                   jax.ShapeDtypeStruct((DEPTH, 1, HY_W), F32),
                   jax.ShapeDtypeStruct((DEPTH, 1, HY_W), F32)],
        scratch_shapes=[pltpu.VMEM((L + 8, 128), F32), pltpu.VMEM((L, HY_FF), F32)],
        compiler_params=_cp(("parallel", "arbitrary")),
        name="hyena_filter_taps",
    )(z, w1, r3(p["hy_ffn_b1"]), p["hy_ffn_w2"], r3(p["hy_ffn_b2"]), p["hy_ffn_w3"], p["hy_ffn_w3"],
      r3(p["hy_ffn_b3"]), r3(p["hy_ffn_b3"]), r3(p["hy_sin_freq"]), r3(p["hy_decay"]))
    tk = min(DFT_TILE, L)
    full = lambda: pl.BlockSpec((None, L, HY_W), lambda kt, l: (l, 0, 0))
    one = lambda: pl.BlockSpec((None, 1, HY_W), lambda kt, l: (l, 0, 0))
    hc, hs = pl.pallas_call(
        functools.partial(_hy_filt_b_kernel, tk=tk),
        grid=(L // tk, DEPTH),
        in_specs=[pl.BlockSpec((tk, L), lambda kt, l: (kt, 0)),
                  pl.BlockSpec((tk, L), lambda kt, l: (kt, 0)),
                  full(), full(), one(), one()],
        out_specs=[pl.BlockSpec((None, tk, HY_W), lambda kt, l: (l, kt, 0)),
                   pl.BlockSpec((None, tk, HY_W), lambda kt, l: (l, kt, 0))],
        out_shape=[jax.ShapeDtypeStruct((DEPTH, L, HY_W), F32)] * 2,
        compiler_params=_cp(("parallel", "parallel")),
        name="hyena_filter_dft",
    )(cos, s_fwd, fp, fm, last, nyq)
    return hc, hs


def _hy_pre_kernel(u0_ref, u1_ref, u2_ref, w0_ref, w1_ref, w2_ref, b0_ref, b1_ref, b2_ref,
                   zz_ref, x0_ref, pad_ref, *, L):
    zeros8 = jnp.zeros((8, 128), F32)
    pad_ref[pl.ds(0, 8), :] = zeros8
    pad_ref[pl.ds(L + 8, 8), :] = zeros8

    def conv(u_ref, w_ref, b_ref):
        pad_ref[pl.ds(8, L), :] = u_ref[...].astype(F32)
        w = w_ref[...]
        return (w[0:1] * pad_ref[pl.ds(7, L), :] + w[1:2] * pad_ref[pl.ds(8, L), :]
                + w[2:3] * pad_ref[pl.ds(9, L), :] + b_ref[...])

    x0_ref[...] = conv(u0_ref, w0_ref, b0_ref).astype(BF16)
    x1 = conv(u1_ref, w1_ref, b1_ref)
    v = conv(u2_ref, w2_ref, b2_ref)
    zz_ref[...] = (x1 * v).astype(BF16)


def _hy_fwd_kernel(c_ref, s_ref, zza_ref, zzb_ref, hc_ref, hs_ref, pra_ref, pia_ref, prb_ref, pib_ref,
                   *, tk, L):
    kt = pl.program_id(0)
    zz = jnp.concatenate([zza_ref[...], zzb_ref[...]], axis=1)
    zc2 = _dot(c_ref[...], zz)
    zs2 = _dot(s_ref[...], zz)
    hc = hc_ref[...]
    hs = hs_ref[...]
    row = kt * tk + lax.broadcasted_iota(I32, (tk, HY_W), 0)
    first = row == 0
    wgt = jnp.where(first, 1.0 / (2 * L), 2.0 / (2 * L)).astype(F32)
    for i, (pr_ref, pi_ref) in enumerate(((pra_ref, pia_ref), (prb_ref, pib_ref))):
        zc = zc2[:, i * HY_W:(i + 1) * HY_W]
        zs = zs2[:, i * HY_W:(i + 1) * HY_W]
        ss = zs * hs
        pr_ref[...] = ((zc * hc - jnp.where(first, 0.0, ss)) * wgt).astype(BF16)
        pi_ref[...] = (jnp.where(first, ss, zc * hs + zs * hc) * wgt).astype(BF16)


def _hy_inv_kernel(c_ref, st_ref, pra_ref, pia_ref, prb_ref, pib_ref, zza_ref, x0a_ref, zzb_ref, x0b_ref,
                   skip_ref, oa_ref, ob_ref):
    pr = jnp.concatenate([pra_ref[...], prb_ref[...]], axis=1)
    pi = jnp.concatenate([pia_ref[...], pib_ref[...]], axis=1)
    y2 = _dot(c_ref[...], pr) + _dot(st_ref[...], pi)
    for i, (zz_ref, x0_ref, o_ref) in enumerate(((zza_ref, x0a_ref, oa_ref), (zzb_ref, x0b_ref, ob_ref))):
        zz = zz_ref[...].astype(F32)
        y = y2[:, i * HY_W:(i + 1) * HY_W]
        o_ref[...] = (x0_ref[...].astype(F32) * (y + zz * skip_ref[...])).astype(BF16)


def _hyena_call(l, u_hy, B, L, tabs, hc, hs, p):
    cos, s_fwd, s_inv = tabs
    n = B * L
    nch = HY_W // 128
    cw = p["hy_conv_w"].reshape(DEPTH, 3, 3 * HY_W)
    cb = p["hy_conv_b"].reshape(DEPTH, 1, 3 * HY_W)
    ub = lambda part: pl.BlockSpec((L, 128), lambda b, c: (b, part * nch + c))
    wb = lambda part: pl.BlockSpec((None, 3, 128), lambda b, c: (l, 0, part * nch + c))
    bb = lambda part: pl.BlockSpec((None, 1, 128), lambda b, c: (l, 0, part * nch + c))
    zz, x0 = pl.pallas_call(
        functools.partial(_hy_pre_kernel, L=L),
        grid=(B, nch),
        in_specs=[ub(0), ub(1), ub(2), wb(0), wb(1), wb(2), bb(0), bb(1), bb(2)],
        out_specs=[pl.BlockSpec((L, 128), lambda b, c: (b, c))] * 2,
        out_shape=[jax.ShapeDtypeStruct((n, HY_W), BF16)] * 2,
        scratch_shapes=[pltpu.VMEM((L + 16, 128), F32)],
        compiler_params=_cp(("parallel", "parallel")),
        name="hyena_short_conv",
    )(u_hy, u_hy, u_hy, cw, cw, cw, cb, cb, cb)

    tk = min(DFT_TILE, L)
    nk = L // tk
    hb = B // 2
    tab = lambda: pl.BlockSpec((tk, L), lambda kt, b: (kt, 0))
    seq = lambda off: pl.BlockSpec((L, HY_W), lambda kt, b: (b + off, 0))
    til = lambda off: pl.BlockSpec((tk, HY_W), lambda kt, b: ((b + off) * nk + kt, 0))
    flt = lambda: pl.BlockSpec((None, tk, HY_W), lambda kt, b: (l, kt, 0))
    half = jax.ShapeDtypeStruct((hb * L, HY_W), BF16)
    pr_a, pi_a, pr_b, pi_b = pl.pallas_call(
        functools.partial(_hy_fwd_kernel, tk=tk, L=L),
        grid=(nk, hb),
        in_specs=[tab(), tab(), seq(0), seq(hb), flt(), flt()],
        out_specs=[til(0)] * 4,
        out_shape=[half] * 4,
        compiler_params=_cp(("parallel", "parallel")),
        name="hyena_dft_fwd",
    )(cos, s_fwd, zz, zz, hc, hs)

    y_a, y_b = pl.pallas_call(
        _hy_inv_kernel,
        grid=(nk, hb),
        in_specs=[tab(), tab(), seq(0), seq(0), seq(0), seq(0),
                  til(0), til(0), til(hb), til(hb),
                  pl.BlockSpec((None, 1, HY_W), lambda qt, b: (l, 0, 0))],
        out_specs=[til(0)] * 2,
        out_shape=[half] * 2,
        compiler_params=_cp(("parallel", "parallel")),
        name="hyena_dft_inv",
    )(cos, s_inv, pr_a, pi_a, pr_b, pi_b, zz, x0, zz, x0, p["hy_skip"].reshape(DEPTH, 1, HY_W))
    return jnp.concatenate([y_a, y_b], axis=0)


def _fnet_kernel(u_ref, csb_ref, c_ref, s_ref, o_ref, uc_ref, us_ref):
    @pl.when(pl.program_id(1) == 0)
    def _():
        t = _dot(u_ref[...], csb_ref[...])
        uc_ref[...] = t[:, :FN_W].astype(BF16)
        us_ref[...] = t[:, FN_W:].astype(BF16)

    o_ref[...] = (_dot(c_ref[...], uc_ref[...]) - _dot(s_ref[...], us_ref[...])).astype(BF16)


def _fnet_call(u_fn, B, L, tabs):
    cos, sin, csb = tabs
    tq = min(DFT_TILE, L)
    nq = L // tq
    return pl.pallas_call(
        _fnet_kernel,
        grid=(B, nq),
        in_specs=[pl.BlockSpec((L, FN_W), lambda b, qt: (b, 0)),
                  pl.BlockSpec((FN_W, 2 * FN_W), lambda b, qt: (0, 0)),
                  pl.BlockSpec((tq, L), lambda b, qt: (qt, 0)),
                  pl.BlockSpec((tq, L), lambda b, qt: (qt, 0))],
        out_specs=pl.BlockSpec((tq, FN_W), lambda b, qt: (b * nq + qt, 0)),
        out_shape=jax.ShapeDtypeStruct((B * L, FN_W), BF16),
        scratch_shapes=[pltpu.VMEM((L, FN_W), BF16), pltpu.VMEM((L, FN_W), BF16)],
        compiler_params=_cp(("parallel", "arbitrary")),
        name="fnet_dft2",
    )(u_fn, csb, cos, sin)


def _ctx_attn_kernel(q_ref, k_ref, v_ref, o_ref):
    for h in range(NA_H):
        sl = slice(h * NA_D, (h + 1) * NA_D)
        q = q_ref[:, sl]
        k = k_ref[:, sl].astype(BF16)
        v = v_ref[:, sl].astype(BF16)
        s = _dot_nt(q, k) * ATTN_SCALE
        m = jnp.max(s, axis=-1, keepdims=True)
        e = jnp.exp(s - m)
        den = jnp.sum(e, axis=-1, keepdims=True)
        o_ref[:, sl] = (_dot(e.astype(BF16), v) / den).astype(BF16)


def _ctx_attn_call(q, k, v, B, L):
    blk = lambda: pl.BlockSpec((L, NA_W), lambda b: (b, 0))
    return pl.pallas_call(
        _ctx_attn_kernel,
        grid=(B,),
        in_specs=[blk(), blk(), blk()],
        out_specs=blk(),
        out_shape=jax.ShapeDtypeStruct((B * L, NA_W), BF16),
        compiler_params=_cp(("parallel",)),
        name="context_attention",
    )(q, k, v)


def _nbr_attn_kernel(q_ref, k_ref, v_ref, kc_ref, vc_ref, bias_ref, o_ref, *, rows, band_rows):
    r0 = pl.program_id(1) * NBR_GROUP
    bs = jnp.clip(r0 - WIN_R // 2, 0, rows - band_rows)
    start = pl.multiple_of(bs * GRID_W, GRID_W)
    band = band_rows * GRID_W
    for h in range(NA_H):
        sl = slice(h * NA_D, (h + 1) * NA_D)
        q = q_ref[:, sl]
        kb = k_ref[pl.ds(start, band), sl].astype(BF16)
        vb = v_ref[pl.ds(start, band), sl].astype(BF16)
        kc = kc_ref[:, sl].astype(BF16)
        vc = vc_ref[:, sl].astype(BF16)
        s_loc = _dot_nt(q, kb) * ATTN_SCALE + bias_ref[h]
        s_ctx = _dot_nt(q, kc) * ATTN_SCALE
        m = jnp.maximum(jnp.max(s_loc, axis=-1, keepdims=True), jnp.max(s_ctx, axis=-1, keepdims=True))
        e_loc = jnp.exp(s_loc - m)
        e_ctx = jnp.exp(s_ctx - m)
        den = jnp.sum(e_loc, axis=-1, keepdims=True) + jnp.sum(e_ctx, axis=-1, keepdims=True)
        o = _dot(e_loc.astype(BF16), vb) + _dot(e_ctx.astype(BF16), vc)
        o_ref[:, sl] = (o / den).astype(BF16)


def _nbr_bias_table(rel_bias, rows):
    wr = min(WIN_R, rows)
    band_rows = min(rows, wr + NBR_GROUP - 1)
    n_groups = rows // NBR_GROUP
    cols = np.arange(GRID_W)
    cs = np.clip(cols - WIN_C // 2, 0, GRID_W - WIN_C)
    mask = (cols[None, :] >= cs[:, None]) & (cols[None, :] < cs[:, None] + WIN_C)
    dc = np.clip(cols[None, :] - cols[:, None], -(WIN_C - 1), WIN_C - 1) + WIN_C - 1
    onehot = ((dc[None] == np.arange(2 * WIN_C - 1)[:, None, None]) & mask[None]).astype(np.float32)
    t = jnp.einsum("lhdj,jqk->lhdqk", rel_bias.astype(F32), jnp.asarray(onehot),
                   precision=lax.Precision.HIGHEST)
    t = t + jnp.asarray(np.where(mask, 0.0, MASK_NEG).astype(np.float32))
    masked = jnp.full(t.shape[:2] + t.shape[3:], MASK_NEG, F32)
    kinds = []
    for g in sorted({0, min(1, n_groups - 1), n_groups - 1}):
        r0 = g * NBR_GROUP
        bs = int(np.clip(r0 - wr // 2, 0, rows - band_rows))
        q_rows = []
        for i in range(NBR_GROUP):
            r = r0 + i
            rs = int(np.clip(r - wr // 2, 0, rows - wr))
            blocks = []
            for u in range(band_rows):
                key_row = bs + u
                blocks.append(t[:, :, key_row - r + WIN_R - 1] if rs <= key_row < rs + wr else masked)
            q_rows.append(jnp.concatenate(blocks, axis=-1))
        kinds.append(jnp.concatenate(q_rows, axis=-2))
    while len(kinds) < 3:
        kinds.append(kinds[-1])
    return jnp.stack(kinds, axis=1), band_rows


def _nbr_attn_call(l, q, k, v, ck, cv, bias, band_rows, B, L):
    rows = L // GRID_W
    n_groups = rows // NBR_GROUP
    P = ck.shape[2]
    gq = NBR_GROUP * GRID_W

    def kind(g):
        return jnp.where(g == 0, 0, jnp.where(g == n_groups - 1, 2, 1))

    return pl.pallas_call(
        functools.partial(_nbr_attn_kernel, rows=rows, band_rows=band_rows),
        grid=(B, n_groups),
        in_specs=[pl.BlockSpec((gq, NA_W), lambda b, g: (b * n_groups + g, 0)),
                  pl.BlockSpec((L, NA_W), lambda b, g: (b, 0)),
                  pl.BlockSpec((L, NA_W), lambda b, g: (b, 0)),
                  pl.BlockSpec((None, None, P, NA_W), lambda b, g: (b, l, 0, 0)),
                  pl.BlockSpec((None, None, P, NA_W), lambda b, g: (b, l, 0, 0)),
                  pl.BlockSpec((None, None, NA_H, gq, band_rows * GRID_W),
                               lambda b, g: (l, kind(g), 0, 0, 0))],
        out_specs=pl.BlockSpec((gq, NA_W), lambda b, g: (b * n_groups + g, 0)),
        out_shape=jax.ShapeDtypeStruct((B * L, NA_W), BF16),
        compiler_params=_cp(("parallel", "parallel")),
        name="neighbourhood_attention",
    )(q, k, v, ck, cv, bias)


def _merge_kernel(x_ref, yhy_ref, yfn_ref, yna_ref, mod_ref, gmix_ref, gffn_ref, wg_ref, why_ref,
                  wfn_ref, wna_ref, wout_ref, rw_ref, rb_ref, cnt0_ref,
                  xn_ref, h2_ref, idx_ref, gate_ref, rank_ref, cnt_ref, run_ref, *,
                  cond_base, rows_per_cond, tm):
    i = pl.program_id(0)
    cnd = cond_base + (i * tm) // rows_per_cond

    @pl.when(i == 0)
    def _():
        run_ref[...] = cnt0_ref[...]

    x = x_ref[...]
    h = (_rms(x, gmix_ref[...]) * (1.0 + _mod_row(mod_ref, cnd, 1)) + _mod_row(mod_ref, cnd, 0)).astype(BF16)
    merged = None
    for b, (y_ref, w_ref) in enumerate(((yhy_ref, why_ref), (yfn_ref, wfn_ref), (yna_ref, wna_ref))):
        gate_b = jax.nn.sigmoid(_dot(h, wg_ref[:, b * D:(b + 1) * D]))
        term = gate_b * _dot(y_ref[...], w_ref[...])
        merged = term if merged is None else merged + term
    xn = x + _mod_row(mod_ref, cnd, 2) * _dot(merged.astype(BF16), wout_ref[...])
    xn_ref[...] = xn
    h2 = _rms(xn, gffn_ref[...]) * (1.0 + _mod_row(mod_ref, cnd, 4)) + _mod_row(mod_ref, cnd, 3)
    for cc in range(D // 128):
        h2_ref[pl.ds(cc, tm, stride=8), :] = h2[:, cc * 128:(cc + 1) * 128]

    h2_hi = h2.astype(BF16)
    h2_lo = (h2 - h2_hi.astype(F32)).astype(BF16)
    rw = rw_ref[...]
    rw_hi = rw.astype(BF16)
    rw_lo = (rw - rw_hi.astype(F32)).astype(BF16)
    logits = _dot(h2_hi, rw_hi) + _dot(h2_lo, rw_hi) + _dot(h2_hi, rw_lo) + rb_ref[...]
    lane = lax.broadcasted_iota(I32, (tm, N_EXP), 1)
    lane4 = lax.broadcasted_iota(I32, (tm, TOP_K), 1)
    rem = logits
    vals, idxs = [], []
    for _ in range(TOP_K):
        m = jnp.max(rem, axis=-1, keepdims=True)
        ix = jnp.min(jnp.where(rem == m, lane, N_EXP), axis=-1, keepdims=True)
        vals.append(m)
        idxs.append(ix)
        rem = jnp.where(lane == ix, -jnp.inf, rem)
    es = [jnp.exp(v - vals[0]) for v in vals]
    den = es[0] + es[1] + es[2] + es[3]

    onehot = jnp.zeros((tm, N_EXP), F32)
    for ix in idxs:
        onehot = onehot + (lane == ix).astype(F32)
    r_i = lax.broadcasted_iota(I32, (tm, tm), 0)
    c_i = lax.broadcasted_iota(I32, (tm, tm), 1)
    lower = (c_i < r_i).astype(BF16)
    before = _dot(lower, onehot.astype(BF16)) + run_ref[...]

    idx_o = jnp.zeros((tm, TOP_K), I32)
    gate_o = jnp.zeros((tm, TOP_K), F32)
    rank_o = jnp.zeros((tm, TOP_K), F32)
    for kk in range(TOP_K):
        rk = jnp.sum(jnp.where(lane == idxs[kk], before, 0.0), axis=-1, keepdims=True)
        idx_o = jnp.where(lane4 == kk, idxs[kk], idx_o)
        gate_o = jnp.where(lane4 == kk, es[kk] / den, gate_o)
        rank_o = jnp.where(lane4 == kk, rk, rank_o)
    idx_ref[...] = idx_o
    gate_ref[...] = gate_o
    rank_ref[...] = rank_o.astype(I32)
    run_ref[...] = run_ref[...] + jnp.sum(onehot, axis=0, keepdims=True)
    cnt_ref[...] = run_ref[...]


def _merge_call(l, x, y_hy, y_fn, y_na, mod, wts, cnt0, cond_base, rows_per_cond):
    n = x.shape[0]
    tm = MERGE_TILE
    kern = functools.partial(_merge_kernel, cond_base=cond_base, rows_per_cond=rows_per_cond, tm=tm)
    row = lambda w: pl.BlockSpec((tm, w), lambda i: (i, 0))
    lw = lambda a, b: pl.BlockSpec((None, a, b), lambda i: (l, 0, 0), pipeline_mode=pl.Buffered(1))
    return pl.pallas_call(
        kern,
        grid=(n // tm,),
        in_specs=[row(D), row(HY_W), row(FN_W), row(NA_W),
                  lw(8, N_MOD * D), lw(1, D), lw(1, D),
                  lw(D, 3 * D), lw(HY_W, D), lw(FN_W, D), lw(NA_W, D), lw(D, D),
                  lw(D, N_EXP), lw(1, N_EXP),
                  pl.BlockSpec((1, N_EXP), lambda i: (0, 0))],
        out_specs=[row(D), pl.BlockSpec((tm * 8, 128), lambda i: (i, 0)), row(TOP_K), row(TOP_K), row(TOP_K),
                   pl.BlockSpec((1, N_EXP), lambda i: (0, 0))],
        out_shape=[jax.ShapeDtypeStruct((n, D), F32), jax.ShapeDtypeStruct((n * 8, 128), F32),
                   jax.ShapeDtypeStruct((n, TOP_K), I32), jax.ShapeDtypeStruct((n, TOP_K), F32),
                   jax.ShapeDtypeStruct((n, TOP_K), I32), jax.ShapeDtypeStruct((1, N_EXP), F32)],
        scratch_shapes=[pltpu.VMEM((1, N_EXP), F32)],
        compiler_params=_cp(("arbitrary",)),
        name="merge_router",
    )(x, y_hy, y_fn, y_na, mod, wts["norm_mix"], wts["norm_ffn"], wts["w_g"], wts["w_hy_out"],
      wts["w_fn_out"], wts["w_na_out"], wts["w_out"], wts["router_w"], wts["router_b"], cnt0)


def _dispatch_kernel(slot_ref, h_ref, xs_in_ref, xs_ref, sem, *, tm):
    del xs_in_ref

    def issue(t, carry):
        src = h_ref.at[pl.ds(pl.multiple_of(t * 8, 8), 8)]
        for kk in range(TOP_K):
            s = slot_ref[0, 0, t * TOP_K + kk]
            dst = xs_ref.at[pl.ds(pl.multiple_of(s * 8, 8), 8)]
            pltpu.make_async_copy(src, dst, sem).start(priority=kk % 2)
        return carry

    lax.fori_loop(0, tm, issue, 0)
    for kk in range(TOP_K):
        pltpu.make_async_copy(h_ref, xs_ref.at[pl.ds(0, tm * 8)], sem).wait()


def _dispatch_call(slot, h2, xs):
    n = h2.shape[0] // 8
    tm = ROW_TILE
    slot3 = slot.reshape(n // tm, 1, tm * TOP_K)
    return pl.pallas_call(
        functools.partial(_dispatch_kernel, tm=tm),
        grid=(n // tm,),
        in_specs=[pl.BlockSpec((1, 1, tm * TOP_K), lambda i: (i, 0, 0), memory_space=pltpu.SMEM),
                  pl.BlockSpec((tm * 8, 128), lambda i: (i, 0)),
                  pl.BlockSpec(memory_space=pl.ANY)],
        out_specs=pl.BlockSpec(memory_space=pl.ANY),
        out_shape=jax.ShapeDtypeStruct(xs.shape, xs.dtype),
        scratch_shapes=[pltpu.SemaphoreType.DMA(())],
        input_output_aliases={2: 0},
        compiler_params=_cp(("arbitrary",)),
        name="moe_dispatch",
    )(slot3, h2, xs)


def _moe_kernel(be_ref, nu_ref, x_ref, w1_ref, b1g_ref, b1l_ref, w2_ref, b2_ref, y_ref,
                w1g_s, w1l_s, w2_s):
    j = pl.program_id(0)
    live = j < nu_ref[0]
    new_expert = jnp.logical_or(j == 0, be_ref[j] != be_ref[jnp.maximum(j - 1, 0)])

    @pl.when(jnp.logical_and(live, new_expert))
    def _():
        grp = 2 * DEINT
        r = lax.broadcasted_iota(I32, (grp, grp), 0)
        c = lax.broadcasted_iota(I32, (grp, grp), 1)
        perm = (r == jnp.where(c < DEINT, 2 * c, 2 * (c - DEINT) + 1)).astype(BF16)
        for ch in range(2 * FF // grp):
            t = _dot(w1_ref[:, ch * grp:(ch + 1) * grp].astype(BF16), perm).astype(BF16)
            w1g_s[:, ch * DEINT:(ch + 1) * DEINT] = t[:, :DEINT]
            w1l_s[:, ch * DEINT:(ch + 1) * DEINT] = t[:, DEINT:]
        w2_s[...] = w2_ref[...].astype(BF16)

    @pl.when(live)
    def _():
        x = jnp.concatenate([x_ref[pl.ds(cc, MOE_ROWS, stride=8), :] for cc in range(D // 128)],
                            axis=1).astype(BF16)
        g = jnp.minimum(_dot(x, w1g_s[...]) + b1g_ref[...], SWIGLU_LIMIT)
        lin = jnp.clip(_dot(x, w1l_s[...]) + b1l_ref[...], -SWIGLU_LIMIT, SWIGLU_LIMIT)
        act = g * jax.nn.sigmoid(SWIGLU_ALPHA * g) * (lin + 1.0)
        y = _dot(act.astype(BF16), w2_s[...]) + b2_ref[...]
        for cc in range(D // 128):
            y_ref[pl.ds(cc, MOE_ROWS, stride=8), :] = y[:, cc * 128:(cc + 1) * 128]

    @pl.when(jnp.logical_not(live))
    def _():
        y_ref[...] = jnp.zeros_like(y_ref)


def _moe_call(l, xs, block_expert, n_used, wts):
    n_slots = xs.shape[0] // 8
    R = MOE_ROWS
    bspec = lambda: pl.BlockSpec((None, None, 1, FF), lambda j, be, nu: (l, be[j], 0, 0))
    grid_spec = pltpu.PrefetchScalarGridSpec(
        num_scalar_prefetch=2,
        grid=(n_slots // R,),
        in_specs=[pl.BlockSpec((R * 8, 128), lambda j, be, nu: (jnp.minimum(j, nu[0] - 1), 0)),
                  pl.BlockSpec((None, None, D, 2 * FF), lambda j, be, nu: (l, be[j], 0, 0)),
                  bspec(), bspec(),
                  pl.BlockSpec((None, None, FF, D), lambda j, be, nu: (l, be[j], 0, 0)),
                  bspec()],
        out_specs=pl.BlockSpec((R * 8, 128), lambda j, be, nu: (j, 0)),
        scratch_shapes=[pltpu.VMEM((D, FF), BF16), pltpu.VMEM((D, FF), BF16),
                        pltpu.VMEM((FF, D), BF16)],
    )
    return pl.pallas_call(
        _moe_kernel,
        grid_spec=grid_spec,
        out_shape=jax.ShapeDtypeStruct((n_slots * 8, 128), F32),
        compiler_params=_cp(("arbitrary",)),
        name="moe_experts",
    )(block_expert, n_used, xs, wts["w1"], wts["b1g"], wts["b1l"], wts["w2"], wts["b2"])


def _combine_kernel(slot_ref, xn_ref, gate_ref, mod_ref, fin_ref, ys_ref, o_ref, ybuf, sem, *,
                    cond_base, rows_per_cond, tm, final):
    cnd = cond_base + (pl.program_id(0) * tm) // rows_per_cond

    def issue(t, carry):
        for kk in range(TOP_K):
            s = slot_ref[0, 0, t * TOP_K + kk]
            src = ys_ref.at[pl.ds(pl.multiple_of(s * 8, 8), 8)]
            dst = ybuf.at[kk, pl.ds(pl.multiple_of(t * 8, 8), 8)]
            pltpu.make_async_copy(src, dst, sem).start(priority=kk % 2)
        return carry

    lax.fori_loop(0, tm, issue, 0)
    for kk in range(TOP_K):
        pltpu.make_async_copy(ys_ref.at[pl.ds(0, tm * 8)], ybuf.at[kk], sem).wait()
    gate = gate_ref[...]
    chunks = []
    for cc in range(D // 128):
        part = gate[:, 0:1] * ybuf[0, pl.ds(cc, tm, stride=8), :]
        for kk in range(1, TOP_K):
            part = part + gate[:, kk:kk + 1] * ybuf[kk, pl.ds(cc, tm, stride=8), :]
        chunks.append(part)
    acc = jnp.concatenate(chunks, axis=1)
    out = xn_ref[...] + _mod_row(mod_ref, cnd, 5) * acc
    if final:
        out = _rms(out, fin_ref[...])
    o_ref[...] = out


def _combine_call(l, slot, xn, gate, mod, final_norm, ys, cond_base, rows_per_cond, final):
    n = xn.shape[0]
    tm = ROW_TILE
    slot3 = slot.reshape(n // tm, 1, tm * TOP_K)
    kern = functools.partial(_combine_kernel, cond_base=cond_base, rows_per_cond=rows_per_cond,
                             tm=tm, final=final)
    return pl.pallas_call(
        kern,
        grid=(n // tm,),
        in_specs=[pl.BlockSpec((1, 1, tm * TOP_K), lambda i: (i, 0, 0), memory_space=pltpu.SMEM),
                  pl.BlockSpec((tm, D), lambda i: (i, 0)),
                  pl.BlockSpec((tm, TOP_K), lambda i: (i, 0)),
                  pl.BlockSpec((None, 8, N_MOD * D), lambda i: (l, 0, 0)),
                  pl.BlockSpec((1, D), lambda i: (0, 0)),
                  pl.BlockSpec(memory_space=pl.ANY)],
        out_specs=pl.BlockSpec((tm, D), lambda i: (i, 0)),
        out_shape=jax.ShapeDtypeStruct((n, D), F32),
        scratch_shapes=[pltpu.VMEM((TOP_K, tm * 8, 128), F32), pltpu.SemaphoreType.DMA(())],
        compiler_params=_cp(("arbitrary",)),
        name="moe_combine",
    )(slot3, xn, gate, mod, final_norm.reshape(1, D), ys)


def kernel(x_prompt, x_sample, cache_k, cache_v, c, c_ctx, ada_w, ada_b, norm_mix, norm_ffn, w_in,
           hy_conv_w, hy_conv_b, hy_ffn_w1, hy_ffn_b1, hy_ffn_w2, hy_ffn_b2, hy_ffn_w3, hy_ffn_b3,
           hy_sin_freq, hy_decay, hy_skip, w_hy_out, w_fn_out, w_na_out, na_rel_bias, w_out,
           router_w, router_b, moe_w1, moe_b1, moe_w2, moe_b2, final_norm):
    Bp, Lp, _ = x_prompt.shape
    Bs, Ls, _ = x_sample.shape
    P = cache_k.shape[2]
    assert Bs + 1 <= 8 and Lp % ROW_TILE == 0 and Ls % MERGE_TILE == 0 and Ls % GRID_W == 0
    assert (Bp * Lp) % MERGE_TILE == 0 and Bp % 2 == 0 and Bs % 2 == 0
    n_p, n_s = Bp * Lp, Bs * Ls
    hyp = dict(hy_conv_w=hy_conv_w, hy_conv_b=hy_conv_b, hy_ffn_w1=hy_ffn_w1, hy_ffn_b1=hy_ffn_b1,
               hy_ffn_w2=hy_ffn_w2, hy_ffn_b2=hy_ffn_b2, hy_ffn_w3=hy_ffn_w3, hy_ffn_b3=hy_ffn_b3,
               hy_sin_freq=hy_sin_freq, hy_decay=hy_decay, hy_skip=hy_skip)

    r3 = lambda a: a.reshape(DEPTH, 1, a.shape[-1])
    wts = dict(
        norm_mix=r3(norm_mix), norm_ffn=r3(norm_ffn),
        w_g=w_in[:, :, BR_W:].astype(BF16), w_hy_out=w_hy_out.astype(BF16),
        w_fn_out=w_fn_out.astype(BF16), w_na_out=w_na_out.astype(BF16), w_out=w_out.astype(BF16),
        router_w=router_w, router_b=r3(router_b),
        w1=moe_w1,
        b1g=moe_b1[..., 0::2].reshape(DEPTH, N_EXP, 1, FF),
        b1l=moe_b1[..., 1::2].reshape(DEPTH, N_EXP, 1, FF),
        w2=moe_w2, b2=moe_b2.reshape(DEPTH, N_EXP, 1, D),
    )
    w_in_b = w_in[:, :, :BR_W].astype(BF16)

    cond8 = jnp.zeros((8, D), F32).at[0].set(c_ctx).at[1:1 + Bs].set(c)
    mod = _mod_call(cond8, ada_w, ada_b)

    streams = []
    for (B, L, base, rpc) in ((Bp, Lp, 0, Bp * Lp), (Bs, Ls, 1, Ls)):
        hy_tabs = _hyena_tables(L)
        hc, hs = _hyena_filters(L, hy_tabs, hyp)
        streams.append(dict(B=B, L=L, base=base, rpc=rpc, hy_tabs=hy_tabs, hc=hc, hs=hs,
                            fn_tabs=_fnet_tables(L)))
    rows_s = Ls // GRID_W
    assert rows_s % NBR_GROUP == 0 and rows_s >= WIN_R + NBR_GROUP - 1
    bias_tab, band_rows = _nbr_bias_table(na_rel_bias, rows_s)
    ck = cache_k.reshape(Bs, DEPTH, P, NA_W)
    cv = cache_v.reshape(Bs, DEPTH, P, NA_W)

    n_assign = (n_p + n_s) * TOP_K
    n_blocks = -(-n_assign // MOE_ROWS) + N_EXP
    n_slots = n_blocks * MOE_ROWS

    xs = jnp.zeros((n_slots * 8, 128), F32)
    xs_tok = [x_prompt.reshape(n_p, D), x_sample.reshape(n_s, D)]
    new_k, new_v = [], []
    for l in range(DEPTH):
        merged = []
        cnt = jnp.zeros((1, N_EXP), F32)
        for si, st in enumerate(streams):
            B, L = st["B"], st["L"]
            x = xs_tok[si]
            u_hy, u_fn, q, k, v = _inproj_call(l, x, mod, wts["norm_mix"], w_in_b, st["base"], st["rpc"])
            y_hy = _hyena_call(l, u_hy, B, L, st["hy_tabs"], st["hc"], st["hs"], hyp)
            y_fn = _fnet_call(u_fn, B, L, st["fn_tabs"])
            if si == 0:
                y_na = _ctx_attn_call(q, k, v, B, L)
                new_k.append(k.reshape(B, L, NA_H, NA_D))
                new_v.append(v.reshape(B, L, NA_H, NA_D))
            else:
                y_na = _nbr_attn_call(l, q, k, v, ck, cv, bias_tab, band_rows, B, L)
            xn, h2, idx, gate, rank, cnt = _merge_call(l, x, y_hy, y_fn, y_na, mod, wts, cnt,
                                                       st["base"], st["rpc"])
            merged.append((xn, h2, idx, gate, rank))

        counts = cnt[0].astype(I32)
        blocks_per_e = (counts + MOE_ROWS - 1) // MOE_ROWS
        block_end = jnp.cumsum(blocks_per_e)
        block_start = block_end - blocks_per_e
        block_expert = jnp.minimum(
            jnp.sum(jnp.arange(n_blocks, dtype=I32)[:, None] >= block_end[None, :], axis=1),
            N_EXP - 1).astype(I32)
        n_used = block_end[-1:].astype(I32)

        slots = []
        for (xn, h2, idx, gate, rank) in merged:
            start_of = jnp.sum(jnp.where(idx[..., None] == jnp.arange(N_EXP, dtype=I32), block_start, 0), axis=-1)
            slot = start_of * MOE_ROWS + rank
            slots.append(slot)
            xs = _dispatch_call(slot, h2, xs)
        ys = _moe_call(l, xs, block_expert, n_used, wts)
        for si, st in enumerate(streams):
            xn, h2, idx, gate, rank = merged[si]
            xs_tok[si] = _combine_call(l, slots[si], xn, gate, mod, final_norm, ys,
                                       st["base"], st["rpc"], l == DEPTH - 1)

    y_prompt = xs_tok[0].reshape(Bp, Lp, D)
    y_sample = xs_tok[1].reshape(Bs, Ls, D)
    return (y_prompt, y_sample, jnp.stack(new_k, axis=1), jnp.stack(new_v, axis=1))
```

```python
import functools
import math

import numpy as np
import jax
import jax.numpy as jnp
from jax import lax
from jax.experimental import pallas as pl
from jax.experimental.pallas import tpu as pltpu

F32 = jnp.float32
BF16 = jnp.bfloat16
I32 = jnp.int32

D = 1024
DEPTH = 4
N_MOD = 6
RMS_EPS = 1e-6
HY_W = 384
HY_EMB = 33
HY_EMB_PAD = 128
HY_FF = 64
FN_W = 256
FN_GROUPS = 4
FN_GROUP_DIM = 64
NA_H = 6
NA_D = 64
NA_W = NA_H * NA_D
GRID_W = 64
WIN_R = 8
WIN_C = 16
NBR_GROUP = 4
ATTN_SCALE = NA_D ** -0.5
BR_W = 3 * HY_W + FN_W + 3 * NA_W
N_EXP = 32
TOP_K = 4
FF = 1024
SWIGLU_LIMIT = 7.0
SWIGLU_ALPHA = 1.702
MASK_NEG = -1e30
DEINT = 128

ROW_TILE = 256
MERGE_TILE = 512
MOE_ROWS = 512
DFT_TILE = 512
VMEM_LIMIT = 56 * 1024 * 1024


def _cp(sem, vmem=VMEM_LIMIT):
    return pltpu.CompilerParams(dimension_semantics=sem, vmem_limit_bytes=vmem)


def _dot(a, b):
    return jnp.dot(a, b, preferred_element_type=F32)


def _dot_hi(a, b):
    return jnp.dot(a, b, preferred_element_type=F32, precision=lax.Precision.HIGHEST)


def _dot_nt(a, b):
    return lax.dot_general(a, b, (((1,), (1,)), ((), ())), preferred_element_type=F32)


def _mod_row(mod_ref, cnd, k):
    return mod_ref[pl.ds(cnd, 1), pl.ds(k * D, D)]


def _rms(x, g):
    return x * lax.rsqrt(jnp.mean(x * x, axis=-1, keepdims=True) + RMS_EPS) * g


def _mod_kernel(cond_ref, w_ref, b_ref, o_ref):
    c = cond_ref[...]
    s = c * jax.nn.sigmoid(c)
    o_ref[...] = _dot(s.astype(BF16), w_ref[...].astype(BF16)) + b_ref[...]


def _mod_call(cond8, ada_w, ada_b):
    tn = 1536
    return pl.pallas_call(
        _mod_kernel,
        grid=(DEPTH, N_MOD * D // tn),
        in_specs=[pl.BlockSpec((8, D), lambda l, j: (0, 0)),
                  pl.BlockSpec((None, D, tn), lambda l, j: (l, 0, j)),
                  pl.BlockSpec((None, 1, tn), lambda l, j: (l, 0, j))],
        out_specs=pl.BlockSpec((None, 8, tn), lambda l, j: (l, 0, j)),
        out_shape=jax.ShapeDtypeStruct((DEPTH, 8, N_MOD * D), F32),
        compiler_params=_cp(("parallel", "parallel")),
        name="adaln_mod",
    )(cond8, ada_w, ada_b.reshape(DEPTH, 1, N_MOD * D))


def _inproj_kernel(x_ref, mod_ref, g_ref, w_ref, hy_ref, fn_ref, q_ref, k_ref, v_ref, *,
                   cond_base, rows_per_cond, tm):
    cnd = cond_base + (pl.program_id(0) * tm) // rows_per_cond
    h = _rms(x_ref[...], g_ref[...]) * (1.0 + _mod_row(mod_ref, cnd, 1)) + _mod_row(mod_ref, cnd, 0)
    u = _dot(h.astype(BF16), w_ref[...])
    o1 = 3 * HY_W
    o2 = o1 + FN_W
    o3 = o2 + NA_W
    o4 = o3 + NA_W
    hy_ref[...] = u[:, :o1].astype(BF16)
    fn_ref[...] = u[:, o1:o2].astype(BF16)
    q_ref[...] = u[:, o2:o3].astype(BF16)
    k_ref[...] = u[:, o3:o4]
    v_ref[...] = u[:, o4:]


def _inproj_call(l, x, mod, norm_mix, w_in_b, cond_base, rows_per_cond):
    n = x.shape[0]
    tm = MERGE_TILE
    kern = functools.partial(_inproj_kernel, cond_base=cond_base, rows_per_cond=rows_per_cond, tm=tm)
    widths = (3 * HY_W, FN_W, NA_W, NA_W, NA_W)
    dts = (BF16, BF16, BF16, F32, F32)
    return pl.pallas_call(
        kern,
        grid=(n // tm,),
        in_specs=[pl.BlockSpec((tm, D), lambda i: (i, 0)),
                  pl.BlockSpec((None, 8, N_MOD * D), lambda i: (l, 0, 0)),
                  pl.BlockSpec((None, 1, D), lambda i: (l, 0, 0)),
                  pl.BlockSpec((None, D, BR_W), lambda i: (l, 0, 0))],
        out_specs=[pl.BlockSpec((tm, w), lambda i: (i, 0)) for w in widths],
        out_shape=[jax.ShapeDtypeStruct((n, w), dt) for w, dt in zip(widths, dts)],
        compiler_params=_cp(("parallel",)),
        name="in_proj",
    )(x, mod, norm_mix, w_in_b)


def _cs_tables(L, N):
    k = np.arange(L, dtype=np.int64)[:, None]
    na = np.arange(0, L, 64, dtype=np.int64)[None, :]
    nb = np.arange(64, dtype=np.int64)[None, :]
    ang_a = 2.0 * np.pi * ((k * na) % N).astype(np.float64) / N
    ang_b = 2.0 * np.pi * ((k * nb) % N).astype(np.float64) / N
    ca = jnp.asarray(np.cos(ang_a), F32)[:, :, None]
    sa = jnp.asarray(np.sin(ang_a), F32)[:, :, None]
    cb = jnp.asarray(np.cos(ang_b), F32)[:, None, :]
    sb = jnp.asarray(np.sin(ang_b), F32)[:, None, :]
    cos = (ca * cb - sa * sb).reshape(L, L)
    sin = (sa * cb + ca * sb).reshape(L, L)
    return cos, sin


def _hyena_tables(L):
    cos, sin = _cs_tables(L, 2 * L)
    alt = (1 - 2 * (jnp.arange(L) % 2)).astype(F32)
    row0 = (jnp.arange(L) == 0)
    s_fwd = jnp.where(row0[:, None], alt[None, :], sin)
    s_inv = jnp.where(row0[None, :], alt[:, None], sin)
    return cos.astype(BF16), s_fwd.astype(BF16), s_inv.astype(BF16)


def _fnet_tables(L):
    cos, sin = _cs_tables(L, L)
    g = FN_GROUP_DIM
    kk = np.arange(g)[:, None] * np.arange(g)[None, :]
    ang = 2.0 * np.pi * (kk % g) / g
    scale = 1.0 / math.sqrt(L * g)
    cb = np.kron(np.eye(FN_GROUPS), np.cos(ang)) * scale
    sb = np.kron(np.eye(FN_GROUPS), np.sin(ang)) * scale
    csb = jnp.asarray(np.concatenate([cb, sb], axis=1), F32).astype(BF16)
    return cos.astype(BF16), sin.astype(BF16), csb


def _hy_filt_a_kernel(z_ref, w1_ref, b1_ref, w2_ref, b2_ref, w3a_ref, w3b_ref, b3a_ref, b3b_ref,
                      fr_ref, dec_ref, fp_ref, fm_ref, last_ref, nyq_ref, pad_ref, a_ref, *, L):
    z = z_ref[...]

    @pl.when(pl.program_id(1) == 0)
    def _():
        fr = fr_ref[...]
        a1 = jnp.sin(fr * (_dot_hi(z, w1_ref[...]) + b1_ref[...]))
        a_ref[...] = jnp.sin(fr * (_dot_hi(a1, w2_ref[...]) + b2_ref[...]))

    a = a_ref[...]
    window = jnp.exp(-z[:, 0:1] * jnp.abs(dec_ref[...]))
    f0 = (_dot_hi(a, w3a_ref[...]) + b3a_ref[...]) * window
    f1 = (_dot_hi(a, w3b_ref[...]) + b3b_ref[...]) * window
    norm = (jnp.sum(jnp.abs(f0), axis=0, keepdims=True)
            + jnp.sum(jnp.abs(f1), axis=0, keepdims=True) + 1e-6)
    f0 = f0 / norm
    f1 = f1 / norm
    pad_ref[pl.ds(0, 8), :] = jnp.zeros((8, 128), F32)
    pad_ref[pl.ds(8, L), :] = f1
    f1s = pad_ref[pl.ds(7, L), :]
    last = pad_ref[pl.ds(L + 7, 1), :]
    fp = f0 + f1s
    fp_ref[...] = fp.astype(BF16)
    fm_ref[...] = (f0 - f1s).astype(BF16)
    last_ref[...] = last
    row = lax.broadcasted_iota(I32, (L, 128), 0)
    alt = (1 - 2 * (row & 1)).astype(F32)
    nyq_ref[...] = jnp.sum(alt * fp, axis=0, keepdims=True) + last


def _hy_filt_b_kernel(c_ref, s_ref, fp_ref, fm_ref, last_ref, nyq_ref, hc_ref, hs_ref, *, tk):
    kt = pl.program_id(0)
    row = kt * tk + lax.broadcasted_iota(I32, (tk, HY_W), 0)
    alt = (1 - 2 * (row & 1)).astype(F32)
    hc_ref[...] = _dot(c_ref[...], fp_ref[...]) + alt * last_ref[...]
    hs = _dot(s_ref[...], fm_ref[...])
    hs_ref[...] = jnp.where(row == 0, nyq_ref[...], hs)


def _hyena_filters(L, tabs, p):
    cos, s_fwd, _ = tabs
    t = np.linspace(0.0, 1.0, L, dtype=np.float32)[:, None]
    bands = (HY_EMB - 1) // 2
    omega = (2.0 * math.pi * np.arange(L, dtype=np.float32)[:, None] / L).astype(np.float32)
    f = np.linspace(1e-4, bands - 1, bands, dtype=np.float32)[None, :]
    z = np.concatenate([t, np.cos(f * omega), -np.sin(f * omega)], axis=-1).astype(np.float32)
    z = jnp.asarray(np.pad(z, ((0, 0), (0, HY_EMB_PAD - HY_EMB))))
    w1 = jnp.pad(p["hy_ffn_w1"], ((0, 0), (0, HY_EMB_PAD - HY_EMB), (0, 0)))
    r3 = lambda a: a.reshape(DEPTH, 1, a.shape[-1])
    nch = HY_W // 128
    vec = lambda: pl.BlockSpec((None, 1, HY_FF), lambda l, c: (l, 0, 0))
    fp, fm, last, nyq = pl.pallas_call(
        functools.partial(_hy_filt_a_kernel, L=L),
        grid=(DEPTH, nch),
        in_specs=[pl.BlockSpec((L, HY_EMB_PAD), lambda l, c: (0, 0)),
                  pl.BlockSpec((None, HY_EMB_PAD, HY_FF), lambda l, c: (l, 0, 0)), vec(),
                  pl.BlockSpec((None, HY_FF, HY_FF), lambda l, c: (l, 0, 0)), vec(),
                  pl.BlockSpec((None, HY_FF, 128), lambda l, c: (l, 0, c)),
                  pl.BlockSpec((None, HY_FF, 128), lambda l, c: (l, 0, nch + c)),
                  pl.BlockSpec((None, 1, 128), lambda l, c: (l, 0, c)),
                  pl.BlockSpec((None, 1, 128), lambda l, c: (l, 0, nch + c)),
                  vec(),
                  pl.BlockSpec((None, 1, 128), lambda l, c: (l, 0, c))],
        out_specs=[pl.BlockSpec((None, L, 128), lambda l, c: (l, 0, c)),
                   pl.BlockSpec((None, L, 128), lambda l, c: (l, 0, c)),
                   pl.BlockSpec((None, 1, 128), lambda l, c: (l, 0, c)),
                   pl.BlockSpec((None, 1, 128), lambda l, c: (l, 0, c))],
        out_shape=[jax.ShapeDtypeStruct((DEPTH, L, HY_W), BF16),
                   jax.ShapeDtypeStruct((DEPTH, L, HY_W), BF16),
                   jax.ShapeDtypeStruct((DEPTH, 1, HY_W), F32),
                   jax.ShapeDtypeStruct((DEPTH, 1, HY_W), F32)],
        scratch_shapes=[pltpu.VMEM((L + 8, 128), F32), pltpu.VMEM((L, HY_FF), F32)],
        compiler_params=_cp(("parallel", "arbitrary")),
        name="hyena_filter_taps",
    )(z, w1, r3(p["hy_ffn_b1"]), p["hy_ffn_w2"], r3(p["hy_ffn_b2"]), p["hy_ffn_w3"], p["hy_ffn_w3"],
      r3(p["hy_ffn_b3"]), r3(p["hy_ffn_b3"]), r3(p["hy_sin_freq"]), r3(p["hy_decay"]))
    tk = min(DFT_TILE, L)
    full = lambda: pl.BlockSpec((None, L, HY_W), lambda kt, l: (l, 0, 0))
    one = lambda: pl.BlockSpec((None, 1, HY_W), lambda kt, l: (l, 0, 0))
    hc, hs = pl.pallas_call(
        functools.partial(_hy_filt_b_kernel, tk=tk),
        grid=(L // tk, DEPTH),
        in_specs=[pl.BlockSpec((tk, L), lambda kt, l: (kt, 0)),
                  pl.BlockSpec((tk, L), lambda kt, l: (kt, 0)),
                  full(), full(), one(), one()],
        out_specs=[pl.BlockSpec((None, tk, HY_W), lambda kt, l: (l, kt, 0)),
                   pl.BlockSpec((None, tk, HY_W), lambda kt, l: (l, kt, 0))],
        out_shape=[jax.ShapeDtypeStruct((DEPTH, L, HY_W), F32)] * 2,
        compiler_params=_cp(("parallel", "parallel")),
        name="hyena_filter_dft",
    )(cos, s_fwd, fp, fm, last, nyq)
    return hc, hs


def _hy_pre_kernel(u0_ref, u1_ref, u2_ref, w0_ref, w1_ref, w2_ref, b0_ref, b1_ref, b2_ref,
                   zz_ref, x0_ref, pad_ref, *, L):
    zeros8 = jnp.zeros((8, 128), F32)
    pad_ref[pl.ds(0, 8), :] = zeros8
    pad_ref[pl.ds(L + 8, 8), :] = zeros8

    def conv(u_ref, w_ref, b_ref):
        pad_ref[pl.ds(8, L), :] = u_ref[...].astype(F32)
        w = w_ref[...]
        return (w[0:1] * pad_ref[pl.ds(7, L), :] + w[1:2] * pad_ref[pl.ds(8, L), :]
                + w[2:3] * pad_ref[pl.ds(9, L), :] + b_ref[...])

    x0_ref[...] = conv(u0_ref, w0_ref, b0_ref).astype(BF16)
    x1 = conv(u1_ref, w1_ref, b1_ref)
    v = conv(u2_ref, w2_ref, b2_ref)
    zz_ref[...] = (x1 * v).astype(BF16)


def _hy_fwd_kernel(c_ref, s_ref, zza_ref, zzb_ref, hc_ref, hs_ref, pra_ref, pia_ref, prb_ref, pib_ref,
                   *, tk, L):
    kt = pl.program_id(0)
    zz = jnp.concatenate([zza_ref[...], zzb_ref[...]], axis=1)
    zc2 = _dot(c_ref[...], zz)
    zs2 = _dot(s_ref[...], zz)
    hc = hc_ref[...]
    hs = hs_ref[...]
    row = kt * tk + lax.broadcasted_iota(I32, (tk, HY_W), 0)
    first = row == 0
    wgt = jnp.where(first, 1.0 / (2 * L), 2.0 / (2 * L)).astype(F32)
    for i, (pr_ref, pi_ref) in enumerate(((pra_ref, pia_ref), (prb_ref, pib_ref))):
        zc = zc2[:, i * HY_W:(i + 1) * HY_W]
        zs = zs2[:, i * HY_W:(i + 1) * HY_W]
        ss = zs * hs
        pr_ref[...] = ((zc * hc - jnp.where(first, 0.0, ss)) * wgt).astype(BF16)
        pi_ref[...] = (jnp.where(first, ss, zc * hs + zs * hc) * wgt).astype(BF16)


def _hy_inv_kernel(c_ref, st_ref, pra_ref, pia_ref, prb_ref, pib_ref, zza_ref, x0a_ref, zzb_ref, x0b_ref,
                   skip_ref, oa_ref, ob_ref):
    pr = jnp.concatenate([pra_ref[...], prb_ref[...]], axis=1)
    pi = jnp.concatenate([pia_ref[...], pib_ref[...]], axis=1)
    y2 = _dot(c_ref[...], pr) + _dot(st_ref[...], pi)
    for i, (zz_ref, x0_ref, o_ref) in enumerate(((zza_ref, x0a_ref, oa_ref), (zzb_ref, x0b_ref, ob_ref))):
        zz = zz_ref[...].astype(F32)
        y = y2[:, i * HY_W:(i + 1) * HY_W]
        o_ref[...] = (x0_ref[...].astype(F32) * (y + zz * skip_ref[...])).astype(BF16)


def _hyena_call(l, u_hy, B, L, tabs, hc, hs, p):
    cos, s_fwd, s_inv = tabs
    n = B * L
    nch = HY_W // 128
    cw = p["hy_conv_w"].reshape(DEPTH, 3, 3 * HY_W)
    cb = p["hy_conv_b"].reshape(DEPTH, 1, 3 * HY_W)
    ub = lambda part: pl.BlockSpec((L, 128), lambda b, c: (b, part * nch + c))
    wb = lambda part: pl.BlockSpec((None, 3, 128), lambda b, c: (l, 0, part * nch + c))
    bb = lambda part: pl.BlockSpec((None, 1, 128), lambda b, c: (l, 0, part * nch + c))
    zz, x0 = pl.pallas_call(
        functools.partial(_hy_pre_kernel, L=L),
        grid=(B, nch),
        in_specs=[ub(0), ub(1), ub(2), wb(0), wb(1), wb(2), bb(0), bb(1), bb(2)],
        out_specs=[pl.BlockSpec((L, 128), lambda b, c: (b, c))] * 2,
        out_shape=[jax.ShapeDtypeStruct((n, HY_W), BF16)] * 2,
        scratch_shapes=[pltpu.VMEM((L + 16, 128), F32)],
        compiler_params=_cp(("parallel", "parallel")),
        name="hyena_short_conv",
    )(u_hy, u_hy, u_hy, cw, cw, cw, cb, cb, cb)

    tk = min(DFT_TILE, L)
    nk = L // tk
    hb = B // 2
    tab = lambda: pl.BlockSpec((tk, L), lambda kt, b: (kt, 0))
    seq = lambda off: pl.BlockSpec((L, HY_W), lambda kt, b: (b + off, 0))
    til = lambda off: pl.BlockSpec((tk, HY_W), lambda kt, b: ((b + off) * nk + kt, 0))
    flt = lambda: pl.BlockSpec((None, tk, HY_W), lambda kt, b: (l, kt, 0))
    half = jax.ShapeDtypeStruct((hb * L, HY_W), BF16)
    pr_a, pi_a, pr_b, pi_b = pl.pallas_call(
        functools.partial(_hy_fwd_kernel, tk=tk, L=L),
        grid=(nk, hb),
        in_specs=[tab(), tab(), seq(0), seq(hb), flt(), flt()],
        out_specs=[til(0)] * 4,
        out_shape=[half] * 4,
        compiler_params=_cp(("parallel", "parallel")),
        name="hyena_dft_fwd",
    )(cos, s_fwd, zz, zz, hc, hs)

    y_a, y_b = pl.pallas_call(
        _hy_inv_kernel,
        grid=(nk, hb),
        in_specs=[tab(), tab(), seq(0), seq(0), seq(0), seq(0),
                  til(0), til(0), til(hb), til(hb),
                  pl.BlockSpec((None, 1, HY_W), lambda qt, b: (l, 0, 0))],
        out_specs=[til(0)] * 2,
        out_shape=[half] * 2,
        compiler_params=_cp(("parallel", "parallel")),
        name="hyena_dft_inv",
    )(cos, s_inv, pr_a, pi_a, pr_b, pi_b, zz, x0, zz, x0, p["hy_skip"].reshape(DEPTH, 1, HY_W))
    return y_a, y_b


def _fnet_kernel(u_ref, csb_ref, c_ref, s_ref, o_ref, uc_ref, us_ref):
    @pl.when(pl.program_id(1) == 0)
    def _():
        t = _dot(u_ref[...], csb_ref[...])
        uc_ref[...] = t[:, :FN_W].astype(BF16)
        us_ref[...] = t[:, FN_W:].astype(BF16)

    o_ref[...] = (_dot(c_ref[...], uc_ref[...]) - _dot(s_ref[...], us_ref[...])).astype(BF16)


def _fnet_call(u_fn, B, L, tabs):
    cos, sin, csb = tabs
    tq = min(DFT_TILE, L)
    nq = L // tq
    return pl.pallas_call(
        _fnet_kernel,
        grid=(B, nq),
        in_specs=[pl.BlockSpec((L, FN_W), lambda b, qt: (b, 0)),
                  pl.BlockSpec((FN_W, 2 * FN_W), lambda b, qt: (0, 0)),
                  pl.BlockSpec((tq, L), lambda b, qt: (qt, 0)),
                  pl.BlockSpec((tq, L), lambda b, qt: (qt, 0))],
        out_specs=pl.BlockSpec((tq, FN_W), lambda b, qt: (b * nq + qt, 0)),
        out_shape=jax.ShapeDtypeStruct((B * L, FN_W), BF16),
        scratch_shapes=[pltpu.VMEM((L, FN_W), BF16), pltpu.VMEM((L, FN_W), BF16)],
        compiler_params=_cp(("parallel", "arbitrary")),
        name="fnet_dft2",
    )(u_fn, csb, cos, sin)


def _ctx_attn_kernel(q_ref, k_ref, v_ref, o_ref):
    for h in range(NA_H):
        sl = slice(h * NA_D, (h + 1) * NA_D)
        q = q_ref[:, sl]
        k = k_ref[:, sl].astype(BF16)
        v = v_ref[:, sl].astype(BF16)
        s = _dot_nt(q, k) * ATTN_SCALE
        m = jnp.max(s, axis=-1, keepdims=True)
        e = jnp.exp(s - m)
        den = jnp.sum(e, axis=-1, keepdims=True)
        o_ref[:, sl] = (_dot(e.astype(BF16), v) / den).astype(BF16)


def _ctx_attn_call(q, k, v, B, L):
    blk = lambda: pl.BlockSpec((L, NA_W), lambda b: (b, 0))
    return pl.pallas_call(
        _ctx_attn_kernel,
        grid=(B,),
        in_specs=[blk(), blk(), blk()],
        out_specs=blk(),
        out_shape=jax.ShapeDtypeStruct((B * L, NA_W), BF16),
        compiler_params=_cp(("parallel",)),
        name="context_attention",
    )(q, k, v)


def _nbr_attn_kernel(q_ref, k_ref, v_ref, kc_ref, vc_ref, bias_ref, o_ref, *, rows, band_rows):
    r0 = pl.program_id(1) * NBR_GROUP
    bs = jnp.clip(r0 - WIN_R // 2, 0, rows - band_rows)
    start = pl.multiple_of(bs * GRID_W, GRID_W)
    band = band_rows * GRID_W
    for h in range(NA_H):
        sl = slice(h * NA_D, (h + 1) * NA_D)
        q = q_ref[:, sl]
        kb = k_ref[pl.ds(start, band), sl].astype(BF16)
        vb = v_ref[pl.ds(start, band), sl].astype(BF16)
        kc = kc_ref[:, sl].astype(BF16)
        vc = vc_ref[:, sl].astype(BF16)
        s_loc = _dot_nt(q, kb) * ATTN_SCALE + bias_ref[h]
        s_ctx = _dot_nt(q, kc) * ATTN_SCALE
        m = jnp.maximum(jnp.max(s_loc, axis=-1, keepdims=True), jnp.max(s_ctx, axis=-1, keepdims=True))
        e_loc = jnp.exp(s_loc - m)
        e_ctx = jnp.exp(s_ctx - m)
        den = jnp.sum(e_loc, axis=-1, keepdims=True) + jnp.sum(e_ctx, axis=-1, keepdims=True)
        o = _dot(e_loc.astype(BF16), vb) + _dot(e_ctx.astype(BF16), vc)
        o_ref[:, sl] = (o / den).astype(BF16)


def _nbr_bias_table(rel_bias, rows):
    wr = min(WIN_R, rows)
    band_rows = min(rows, wr + NBR_GROUP - 1)
    n_groups = rows // NBR_GROUP
    cols = np.arange(GRID_W)
    cs = np.clip(cols - WIN_C // 2, 0, GRID_W - WIN_C)
    mask = (cols[None, :] >= cs[:, None]) & (cols[None, :] < cs[:, None] + WIN_C)
    dc = np.clip(cols[None, :] - cols[:, None], -(WIN_C - 1), WIN_C - 1) + WIN_C - 1
    onehot = ((dc[None] == np.arange(2 * WIN_C - 1)[:, None, None]) & mask[None]).astype(np.float32)
    t = jnp.einsum("lhdj,jqk->lhdqk", rel_bias.astype(F32), jnp.asarray(onehot),
                   precision=lax.Precision.HIGHEST)
    t = t + jnp.asarray(np.where(mask, 0.0, MASK_NEG).astype(np.float32))
    masked = jnp.full(t.shape[:2] + t.shape[3:], MASK_NEG, F32)
    kinds = []
    for g in sorted({0, min(1, n_groups - 1), n_groups - 1}):
        r0 = g * NBR_GROUP
        bs = int(np.clip(r0 - wr // 2, 0, rows - band_rows))
        q_rows = []
        for i in range(NBR_GROUP):
            r = r0 + i
            rs = int(np.clip(r - wr // 2, 0, rows - wr))
            blocks = []
            for u in range(band_rows):
                key_row = bs + u
                blocks.append(t[:, :, key_row - r + WIN_R - 1] if rs <= key_row < rs + wr else masked)
            q_rows.append(jnp.concatenate(blocks, axis=-1))
        kinds.append(jnp.concatenate(q_rows, axis=-2))
    while len(kinds) < 3:
        kinds.append(kinds[-1])
    return jnp.stack(kinds, axis=1), band_rows


def _nbr_attn_call(l, q, k, v, ck, cv, bias, band_rows, B, L):
    rows = L // GRID_W
    n_groups = rows // NBR_GROUP
    P = ck.shape[2]
    gq = NBR_GROUP * GRID_W

    def kind(g):
        return jnp.where(g == 0, 0, jnp.where(g == n_groups - 1, 2, 1))

    return pl.pallas_call(
        functools.partial(_nbr_attn_kernel, rows=rows, band_rows=band_rows),
        grid=(B, n_groups),
        in_specs=[pl.BlockSpec((gq, NA_W), lambda b, g: (b * n_groups + g, 0)),
                  pl.BlockSpec((L, NA_W), lambda b, g: (b, 0)),
                  pl.BlockSpec((L, NA_W), lambda b, g: (b, 0)),
                  pl.BlockSpec((None, None, P, NA_W), lambda b, g: (b, l, 0, 0)),
                  pl.BlockSpec((None, None, P, NA_W), lambda b, g: (b, l, 0, 0)),
                  pl.BlockSpec((None, None, NA_H, gq, band_rows * GRID_W),
                               lambda b, g: (l, kind(g), 0, 0, 0))],
        out_specs=pl.BlockSpec((gq, NA_W), lambda b, g: (b * n_groups + g, 0)),
        out_shape=jax.ShapeDtypeStruct((B * L, NA_W), BF16),
        compiler_params=_cp(("parallel", "parallel")),
        name="neighbourhood_attention",
    )(q, k, v, ck, cv, bias)


def _merge_kernel(x_ref, yhya_ref, yhyb_ref, yfn_ref, yna_ref, mod_ref, gmix_ref, gffn_ref, wg_ref, why_ref,
                  wfn_ref, wna_ref, wout_ref, rw_ref, rb_ref, cnt0_ref,
                  xn_ref, h2_ref, idx_ref, gate_ref, rank_ref, cnt_ref, run_ref, *,
                  cond_base, rows_per_cond, tm, half_tiles):
    i = pl.program_id(0)
    cnd = cond_base + (i * tm) // rows_per_cond

    @pl.when(i == 0)
    def _():
        run_ref[...] = cnt0_ref[...]

    y_hy = jnp.where(i < half_tiles, yhya_ref[...], yhyb_ref[...])

    x = x_ref[...]
    h = (_rms(x, gmix_ref[...]) * (1.0 + _mod_row(mod_ref, cnd, 1)) + _mod_row(mod_ref, cnd, 0)).astype(BF16)
    merged = None
    for b, (y_b, w_ref) in enumerate(((y_hy, why_ref), (yfn_ref[...], wfn_ref), (yna_ref[...], wna_ref))):
        gate_b = jax.nn.sigmoid(_dot(h, wg_ref[:, b * D:(b + 1) * D]))
        term = gate_b * _dot(y_b, w_ref[...])
        merged = term if merged is None else merged + term
    xn = x + _mod_row(mod_ref, cnd, 2) * _dot(merged.astype(BF16), wout_ref[...])
    xn_ref[...] = xn
    h2 = _rms(xn, gffn_ref[...]) * (1.0 + _mod_row(mod_ref, cnd, 4)) + _mod_row(mod_ref, cnd, 3)
    for cc in range(D // 128):
        h2_ref[pl.ds(cc, tm, stride=8), :] = h2[:, cc * 128:(cc + 1) * 128]

    h2_hi = h2.astype(BF16)
    h2_lo = (h2 - h2_hi.astype(F32)).astype(BF16)
    rw = rw_ref[...]
    rw_hi = rw.astype(BF16)
    rw_lo = (rw - rw_hi.astype(F32)).astype(BF16)
    logits = _dot(h2_hi, rw_hi) + _dot(h2_lo, rw_hi) + _dot(h2_hi, rw_lo) + rb_ref[...]
    lane = lax.broadcasted_iota(I32, (tm, N_EXP), 1)
    lane4 = lax.broadcasted_iota(I32, (tm, TOP_K), 1)
    rem = logits
    vals, idxs = [], []
    for _ in range(TOP_K):
        m = jnp.max(rem, axis=-1, keepdims=True)
        ix = jnp.min(jnp.where(rem == m, lane, N_EXP), axis=-1, keepdims=True)
        vals.append(m)
        idxs.append(ix)
        rem = jnp.where(lane == ix, -jnp.inf, rem)
    es = [jnp.exp(v - vals[0]) for v in vals]
    den = es[0] + es[1] + es[2] + es[3]

    onehot = jnp.zeros((tm, N_EXP), F32)
    for ix in idxs:
        onehot = onehot + (lane == ix).astype(F32)
    r_i = lax.broadcasted_iota(I32, (tm, tm), 0)
    c_i = lax.broadcasted_iota(I32, (tm, tm), 1)
    lower = (c_i < r_i).astype(BF16)
    before = _dot(lower, onehot.astype(BF16)) + run_ref[...]

    idx_o = jnp.zeros((tm, TOP_K), I32)
    gate_o = jnp.zeros((tm, TOP_K), F32)
    rank_o = jnp.zeros((tm, TOP_K), F32)
    for kk in range(TOP_K):
        rk = jnp.sum(jnp.where(lane == idxs[kk], before, 0.0), axis=-1, keepdims=True)
        idx_o = jnp.where(lane4 == kk, idxs[kk], idx_o)
        gate_o = jnp.where(lane4 == kk, es[kk] / den, gate_o)
        rank_o = jnp.where(lane4 == kk, rk, rank_o)
    idx_ref[...] = idx_o
    gate_ref[...] = gate_o
    rank_ref[...] = rank_o.astype(I32)
    run_ref[...] = run_ref[...] + jnp.sum(onehot, axis=0, keepdims=True)
    cnt_ref[...] = run_ref[...]


def _merge_call(l, x, y_hy, y_fn, y_na, mod, wts, cnt0, cond_base, rows_per_cond):
    n = x.shape[0]
    tm = MERGE_TILE
    y_hy_a, y_hy_b = y_hy
    half_tiles = y_hy_a.shape[0] // tm
    assert y_hy_a.shape[0] % tm == 0
    kern = functools.partial(_merge_kernel, cond_base=cond_base, rows_per_cond=rows_per_cond, tm=tm,
                             half_tiles=half_tiles)
    row = lambda w: pl.BlockSpec((tm, w), lambda i: (i, 0))
    lw = lambda a, b: pl.BlockSpec((None, a, b), lambda i: (l, 0, 0), pipeline_mode=pl.Buffered(1))
    return pl.pallas_call(
        kern,
        grid=(n // tm,),
        in_specs=[row(D),
                  pl.BlockSpec((tm, HY_W), lambda i: (jnp.minimum(i, half_tiles - 1), 0)),
                  pl.BlockSpec((tm, HY_W), lambda i: (jnp.maximum(i - half_tiles, 0), 0)),
                  row(FN_W), row(NA_W),
                  lw(8, N_MOD * D), lw(1, D), lw(1, D),
                  lw(D, 3 * D), lw(HY_W, D), lw(FN_W, D), lw(NA_W, D), lw(D, D),
                  lw(D, N_EXP), lw(1, N_EXP),
                  pl.BlockSpec((1, N_EXP), lambda i: (0, 0))],
        out_specs=[row(D), pl.BlockSpec((tm * 8, 128), lambda i: (i, 0)), row(TOP_K), row(TOP_K), row(TOP_K),
                   pl.BlockSpec((1, N_EXP), lambda i: (0, 0))],
        out_shape=[jax.ShapeDtypeStruct((n, D), F32), jax.ShapeDtypeStruct((n * 8, 128), F32),
                   jax.ShapeDtypeStruct((n, TOP_K), I32), jax.ShapeDtypeStruct((n, TOP_K), F32),
                   jax.ShapeDtypeStruct((n, TOP_K), I32), jax.ShapeDtypeStruct((1, N_EXP), F32)],
        scratch_shapes=[pltpu.VMEM((1, N_EXP), F32)],
        compiler_params=_cp(("arbitrary",)),
        name="merge_router",
    )(x, y_hy_a, y_hy_b, y_fn, y_na, mod, wts["norm_mix"], wts["norm_ffn"], wts["w_g"], wts["w_hy_out"],
      wts["w_fn_out"], wts["w_na_out"], wts["w_out"], wts["router_w"], wts["router_b"], cnt0)


def _dispatch_kernel(slot_ref, h_ref, xs_in_ref, xs_ref, sem, *, tm):
    del xs_in_ref

    def issue(t, carry):
        src = h_ref.at[pl.ds(pl.multiple_of(t * 8, 8), 8)]
        for kk in range(TOP_K):
            s = slot_ref[0, 0, t * TOP_K + kk]
            dst = xs_ref.at[pl.ds(pl.multiple_of(s * 8, 8), 8)]
            pltpu.make_async_copy(src, dst, sem).start(priority=kk % 2)
        return carry

    lax.fori_loop(0, tm, issue, 0)
    for kk in range(TOP_K):
        pltpu.make_async_copy(h_ref, xs_ref.at[pl.ds(0, tm * 8)], sem).wait()


def _dispatch_call(slot, h2, xs):
    n = h2.shape[0] // 8
    tm = ROW_TILE
    slot3 = slot.reshape(n // tm, 1, tm * TOP_K)
    return pl.pallas_call(
        functools.partial(_dispatch_kernel, tm=tm),
        grid=(n // tm,),
        in_specs=[pl.BlockSpec((1, 1, tm * TOP_K), lambda i: (i, 0, 0), memory_space=pltpu.SMEM),
                  pl.BlockSpec((tm * 8, 128), lambda i: (i, 0)),
                  pl.BlockSpec(memory_space=pl.ANY)],
        out_specs=pl.BlockSpec(memory_space=pl.ANY),
        out_shape=jax.ShapeDtypeStruct(xs.shape, xs.dtype),
        scratch_shapes=[pltpu.SemaphoreType.DMA(())],
        input_output_aliases={2: 0},
        compiler_params=_cp(("arbitrary",)),
        name="moe_dispatch",
    )(slot3, h2, xs)


def _moe_kernel(be_ref, nu_ref, x_ref, w1_ref, b1g_ref, b1l_ref, w2_ref, b2_ref, y_ref,
                w1g_s, w1l_s, w2_s):
    j = pl.program_id(0)
    live = j < nu_ref[0]
    new_expert = jnp.logical_or(j == 0, be_ref[j] != be_ref[jnp.maximum(j - 1, 0)])

    @pl.when(jnp.logical_and(live, new_expert))
    def _():
        grp = 2 * DEINT
        r = lax.broadcasted_iota(I32, (grp, grp), 0)
        c = lax.broadcasted_iota(I32, (grp, grp), 1)
        perm = (r == jnp.where(c < DEINT, 2 * c, 2 * (c - DEINT) + 1)).astype(BF16)
        for ch in range(2 * FF // grp):
            t = _dot(w1_ref[:, ch * grp:(ch + 1) * grp].astype(BF16), perm).astype(BF16)
            w1g_s[:, ch * DEINT:(ch + 1) * DEINT] = t[:, :DEINT]
            w1l_s[:, ch * DEINT:(ch + 1) * DEINT] = t[:, DEINT:]
        w2_s[...] = w2_ref[...].astype(BF16)

    @pl.when(live)
    def _():
        x = jnp.concatenate([x_ref[pl.ds(cc, MOE_ROWS, stride=8), :] for cc in range(D // 128)],
                            axis=1).astype(BF16)
        g = jnp.minimum(_dot(x, w1g_s[...]) + b1g_ref[...], SWIGLU_LIMIT)
        lin = jnp.clip(_dot(x, w1l_s[...]) + b1l_ref[...], -SWIGLU_LIMIT, SWIGLU_LIMIT)
        act = g * jax.nn.sigmoid(SWIGLU_ALPHA * g) * (lin + 1.0)
        y = _dot(act.astype(BF16), w2_s[...]) + b2_ref[...]
        for cc in range(D // 128):
            y_ref[pl.ds(cc, MOE_ROWS, stride=8), :] = y[:, cc * 128:(cc + 1) * 128]

    @pl.when(jnp.logical_not(live))
    def _():
        y_ref[...] = jnp.zeros_like(y_ref)


def _moe_call(l, xs, block_expert, n_used, wts):
    n_slots = xs.shape[0] // 8
    R = MOE_ROWS
    bspec = lambda: pl.BlockSpec((None, None, 1, FF), lambda j, be, nu: (l, be[j], 0, 0))
    grid_spec = pltpu.PrefetchScalarGridSpec(
        num_scalar_prefetch=2,
        grid=(n_slots // R,),
        in_specs=[pl.BlockSpec((R * 8, 128), lambda j, be, nu: (jnp.minimum(j, nu[0] - 1), 0)),
                  pl.BlockSpec((None, None, D, 2 * FF), lambda j, be, nu: (l, be[j], 0, 0)),
                  bspec(), bspec(),
                  pl.BlockSpec((None, None, FF, D), lambda j, be, nu: (l, be[j], 0, 0)),
                  bspec()],
        out_specs=pl.BlockSpec((R * 8, 128), lambda j, be, nu: (j, 0)),
        scratch_shapes=[pltpu.VMEM((D, FF), BF16), pltpu.VMEM((D, FF), BF16),
                        pltpu.VMEM((FF, D), BF16)],
    )
    return pl.pallas_call(
        _moe_kernel,
        grid_spec=grid_spec,
        out_shape=jax.ShapeDtypeStruct((n_slots * 8, 128), F32),
        compiler_params=_cp(("arbitrary",)),
        name="moe_experts",
    )(block_expert, n_used, xs, wts["w1"], wts["b1g"], wts["b1l"], wts["w2"], wts["b2"])


def _combine_kernel(slot_ref, slot_next_ref, xn_ref, gate_ref, mod_ref, fin_ref, ys_ref, o_ref, ybuf, sem, *,
                    cond_base, rows_per_cond, tm, final):
    i = pl.program_id(0)
    n_steps = pl.num_programs(0)
    cnd = cond_base + (i * tm) // rows_per_cond

    def issue(s_ref, buf):
        def body(t, carry):
            for kk in range(TOP_K):
                s = s_ref[0, 0, t * TOP_K + kk]
                src = ys_ref.at[pl.ds(pl.multiple_of(s * 8, 8), 8)]
                dst = ybuf.at[buf, kk, pl.ds(pl.multiple_of(t * 8, 8), 8)]
                pltpu.make_async_copy(src, dst, sem.at[buf]).start(priority=kk % 2)
            return carry

        lax.fori_loop(0, tm, body, 0)

    def finish(buf):
        for kk in range(TOP_K):
            pltpu.make_async_copy(ys_ref.at[pl.ds(0, tm * 8)], ybuf.at[buf, kk], sem.at[buf]).wait()
        gate = gate_ref[...]
        chunks = []
        for cc in range(D // 128):
            part = gate[:, 0:1] * ybuf[buf, 0, pl.ds(cc, tm, stride=8), :]
            for kk in range(1, TOP_K):
                part = part + gate[:, kk:kk + 1] * ybuf[buf, kk, pl.ds(cc, tm, stride=8), :]
            chunks.append(part)
        acc = jnp.concatenate(chunks, axis=1)
        out = xn_ref[...] + _mod_row(mod_ref, cnd, 5) * acc
        if final:
            out = _rms(out, fin_ref[...])
        o_ref[...] = out

    @pl.when(i == 0)
    def _():
        issue(slot_ref, 0)

    for buf in range(2):
        @pl.when(jnp.logical_and(i % 2 == buf, i + 1 < n_steps))
        def _():
            issue(slot_next_ref, 1 - buf)

        @pl.when(i % 2 == buf)
        def _():
            finish(buf)


def _combine_call(l, slot, xn, gate, mod, final_norm, ys, cond_base, rows_per_cond, final):
    n = xn.shape[0]
    tm = ROW_TILE
    nt = n // tm
    slot3 = slot.reshape(nt, 1, tm * TOP_K)
    kern = functools.partial(_combine_kernel, cond_base=cond_base, rows_per_cond=rows_per_cond,
                             tm=tm, final=final)
    return pl.pallas_call(
        kern,
        grid=(nt,),
        in_specs=[pl.BlockSpec((1, 1, tm * TOP_K), lambda i: (i, 0, 0), memory_space=pltpu.SMEM),
                  pl.BlockSpec((1, 1, tm * TOP_K), lambda i: (jnp.minimum(i + 1, nt - 1), 0, 0),
                               memory_space=pltpu.SMEM),
                  pl.BlockSpec((tm, D), lambda i: (i, 0)),
                  pl.BlockSpec((tm, TOP_K), lambda i: (i, 0)),
                  pl.BlockSpec((None, 8, N_MOD * D), lambda i: (l, 0, 0)),
                  pl.BlockSpec((1, D), lambda i: (0, 0)),
                  pl.BlockSpec(memory_space=pl.ANY)],
        out_specs=pl.BlockSpec((tm, D), lambda i: (i, 0)),
        out_shape=jax.ShapeDtypeStruct((n, D), F32),
        scratch_shapes=[pltpu.VMEM((2, TOP_K, tm * 8, 128), F32), pltpu.SemaphoreType.DMA((2,))],
        compiler_params=_cp(("arbitrary",)),
        name="moe_combine",
    )(slot3, slot3, xn, gate, mod, final_norm.reshape(1, D), ys)


def kernel(x_prompt, x_sample, cache_k, cache_v, c, c_ctx, ada_w, ada_b, norm_mix, norm_ffn, w_in,
           hy_conv_w, hy_conv_b, hy_ffn_w1, hy_ffn_b1, hy_ffn_w2, hy_ffn_b2, hy_ffn_w3, hy_ffn_b3,
           hy_sin_freq, hy_decay, hy_skip, w_hy_out, w_fn_out, w_na_out, na_rel_bias, w_out,
           router_w, router_b, moe_w1, moe_b1, moe_w2, moe_b2, final_norm):
    Bp, Lp, _ = x_prompt.shape
    Bs, Ls, _ = x_sample.shape
    P = cache_k.shape[2]
    assert Bs + 1 <= 8 and Lp % ROW_TILE == 0 and Ls % MERGE_TILE == 0 and Ls % GRID_W == 0
    assert (Bp * Lp) % MERGE_TILE == 0 and Bp % 2 == 0 and Bs % 2 == 0
    n_p, n_s = Bp * Lp, Bs * Ls
    hyp = dict(hy_conv_w=hy_conv_w, hy_conv_b=hy_conv_b, hy_ffn_w1=hy_ffn_w1, hy_ffn_b1=hy_ffn_b1,
               hy_ffn_w2=hy_ffn_w2, hy_ffn_b2=hy_ffn_b2, hy_ffn_w3=hy_ffn_w3, hy_ffn_b3=hy_ffn_b3,
               hy_sin_freq=hy_sin_freq, hy_decay=hy_decay, hy_skip=hy_skip)

    r3 = lambda a: a.reshape(DEPTH, 1, a.shape[-1])
    wts = dict(
        norm_mix=r3(norm_mix), norm_ffn=r3(norm_ffn),
        w_g=w_in[:, :, BR_W:].astype(BF16), w_hy_out=w_hy_out.astype(BF16),
        w_fn_out=w_fn_out.astype(BF16), w_na_out=w_na_out.astype(BF16), w_out=w_out.astype(BF16),
        router_w=router_w, router_b=r3(router_b),
        w1=moe_w1,
        b1g=moe_b1[..., 0::2].reshape(DEPTH, N_EXP, 1, FF),
        b1l=moe_b1[..., 1::2].reshape(DEPTH, N_EXP, 1, FF),
        w2=moe_w2, b2=moe_b2.reshape(DEPTH, N_EXP, 1, D),
    )
    w_in_b = w_in[:, :, :BR_W].astype(BF16)

    cond8 = jnp.zeros((8, D), F32).at[0].set(c_ctx).at[1:1 + Bs].set(c)
    mod = _mod_call(cond8, ada_w, ada_b)

    streams = []
    for (B, L, base, rpc) in ((Bp, Lp, 0, Bp * Lp), (Bs, Ls, 1, Ls)):
        hy_tabs = _hyena_tables(L)
        hc, hs = _hyena_filters(L, hy_tabs, hyp)
        streams.append(dict(B=B, L=L, base=base, rpc=rpc, hy_tabs=hy_tabs, hc=hc, hs=hs,
                            fn_tabs=_fnet_tables(L)))
    rows_s = Ls // GRID_W
    assert rows_s % NBR_GROUP == 0 and rows_s >= WIN_R + NBR_GROUP - 1
    bias_tab, band_rows = _nbr_bias_table(na_rel_bias, rows_s)
    ck = cache_k.reshape(Bs, DEPTH, P, NA_W)
    cv = cache_v.reshape(Bs, DEPTH, P, NA_W)

    n_assign = (n_p + n_s) * TOP_K
    n_blocks = -(-n_assign // MOE_ROWS) + N_EXP
    n_slots = n_blocks * MOE_ROWS

    xs = jnp.zeros((n_slots * 8, 128), F32)
    xs_tok = [x_prompt.reshape(n_p, D), x_sample.reshape(n_s, D)]
    new_k, new_v = [], []
    for l in range(DEPTH):
        merged = []
        cnt = jnp.zeros((1, N_EXP), F32)
        for si, st in enumerate(streams):
            B, L = st["B"], st["L"]
            x = xs_tok[si]
            u_hy, u_fn, q, k, v = _inproj_call(l, x, mod, wts["norm_mix"], w_in_b, st["base"], st["rpc"])
            y_hy = _hyena_call(l, u_hy, B, L, st["hy_tabs"], st["hc"], st["hs"], hyp)
            y_fn = _fnet_call(u_fn, B, L, st["fn_tabs"])
            if si == 0:
                y_na = _ctx_attn_call(q, k, v, B, L)
                new_k.append(k.reshape(B, L, NA_H, NA_D))
                new_v.append(v.reshape(B, L, NA_H, NA_D))
            else:
                y_na = _nbr_attn_call(l, q, k, v, ck, cv, bias_tab, band_rows, B, L)
            xn, h2, idx, gate, rank, cnt = _merge_call(l, x, y_hy, y_fn, y_na, mod, wts, cnt,
                                                       st["base"], st["rpc"])
            merged.append((xn, h2, idx, gate, rank))

        counts = cnt[0].astype(I32)
        blocks_per_e = (counts + MOE_ROWS - 1) // MOE_ROWS
        block_end = jnp.cumsum(blocks_per_e)
        block_start = block_end - blocks_per_e
        block_expert = jnp.minimum(
            jnp.sum(jnp.arange(n_blocks, dtype=I32)[:, None] >= block_end[None, :], axis=1),
            N_EXP - 1).astype(I32)
        n_used = block_end[-1:].astype(I32)

        slots = []
        for (xn, h2, idx, gate, rank) in merged:
            start_of = jnp.sum(jnp.where(idx[..., None] == jnp.arange(N_EXP, dtype=I32), block_start, 0), axis=-1)
            slot = start_of * MOE_ROWS + rank
            slots.append(slot)
            xs = _dispatch_call(slot, h2, xs)
        ys = _moe_call(l, xs, block_expert, n_used, wts)
        for si, st in enumerate(streams):
            xn, h2, idx, gate, rank = merged[si]
            xs_tok[si] = _combine_call(l, slots[si], xn, gate, mod, final_norm, ys,
                                       st["base"], st["rpc"], l == DEPTH - 1)

    y_prompt = xs_tok[0].reshape(Bp, Lp, D)
    y_sample = xs_tok[1].reshape(Bs, Ls, D)
    return (y_prompt, y_sample, jnp.stack(new_k, axis=1), jnp.stack(new_v, axis=1))
```

```python
import functools
import math

import numpy as np
import jax
import jax.numpy as jnp
from jax import lax
from jax.experimental import pallas as pl
from jax.experimental.pallas import tpu as pltpu

F32 = jnp.float32
BF16 = jnp.bfloat16
I32 = jnp.int32

D = 1024
DEPTH = 4
N_MOD = 6
RMS_EPS = 1e-6
HY_W = 384
HY_EMB = 33
HY_EMB_PAD = 128
HY_FF = 64
FN_W = 256
FN_GROUPS = 4
FN_GROUP_DIM = 64
NA_H = 6
NA_D = 64
NA_W = NA_H * NA_D
GRID_W = 64
WIN_R = 8
WIN_C = 16
NBR_GROUP = 4
ATTN_SCALE = NA_D ** -0.5
BR_W = 3 * HY_W + FN_W + 3 * NA_W
N_EXP = 32
TOP_K = 4
FF = 1024
SWIGLU_LIMIT = 7.0
SWIGLU_ALPHA = 1.702
MASK_NEG = -1e30
DEINT = 128

ROW_TILE = 256
MERGE_TILE = 512
MOE_ROWS = 512
DFT_TILE = 512
VMEM_LIMIT = 56 * 1024 * 1024


def _cp(sem, vmem=VMEM_LIMIT):
    return pltpu.CompilerParams(dimension_semantics=sem, vmem_limit_bytes=vmem)


def _dot(a, b):
    return jnp.dot(a, b, preferred_element_type=F32)


def _dot_hi(a, b):
    return jnp.dot(a, b, preferred_element_type=F32, precision=lax.Precision.HIGHEST)


def _dot_nt(a, b):
    return lax.dot_general(a, b, (((1,), (1,)), ((), ())), preferred_element_type=F32)


def _mod_row(mod_ref, cnd, k):
    return mod_ref[pl.ds(cnd, 1), pl.ds(k * D, D)]


def _rms(x, g):
    return x * lax.rsqrt(jnp.mean(x * x, axis=-1, keepdims=True) + RMS_EPS) * g


def _mod_kernel(cond_ref, w_ref, b_ref, o_ref):
    c = cond_ref[...]
    s = c * jax.nn.sigmoid(c)
    o_ref[...] = _dot(s.astype(BF16), w_ref[...].astype(BF16)) + b_ref[...]


def _mod_call(cond8, ada_w, ada_b):
    tn = 1536
    return pl.pallas_call(
        _mod_kernel,
        grid=(DEPTH, N_MOD * D // tn),
        in_specs=[pl.BlockSpec((8, D), lambda l, j: (0, 0)),
                  pl.BlockSpec((None, D, tn), lambda l, j: (l, 0, j)),
                  pl.BlockSpec((None, 1, tn), lambda l, j: (l, 0, j))],
        out_specs=pl.BlockSpec((None, 8, tn), lambda l, j: (l, 0, j)),
        out_shape=jax.ShapeDtypeStruct((DEPTH, 8, N_MOD * D), F32),
        compiler_params=_cp(("parallel", "parallel")),
        name="adaln_mod",
    )(cond8, ada_w, ada_b.reshape(DEPTH, 1, N_MOD * D))


def _inproj_kernel(x_ref, mod_ref, g_ref, w_ref, hy_ref, fn_ref, q_ref, k_ref, v_ref, *,
                   cond_base, rows_per_cond, tm):
    cnd = cond_base + (pl.program_id(0) * tm) // rows_per_cond
    h = _rms(x_ref[...], g_ref[...]) * (1.0 + _mod_row(mod_ref, cnd, 1)) + _mod_row(mod_ref, cnd, 0)
    u = _dot(h.astype(BF16), w_ref[...])
    o1 = 3 * HY_W
    o2 = o1 + FN_W
    o3 = o2 + NA_W
    o4 = o3 + NA_W
    hy_ref[...] = u[:, :o1].astype(BF16)
    fn_ref[...] = u[:, o1:o2].astype(BF16)
    q_ref[...] = u[:, o2:o3].astype(BF16)
    k_ref[...] = u[:, o3:o4]
    v_ref[...] = u[:, o4:]


def _inproj_call(l, x, mod, norm_mix, w_in_b, cond_base, rows_per_cond):
    n = x.shape[0]
    tm = MERGE_TILE
    kern = functools.partial(_inproj_kernel, cond_base=cond_base, rows_per_cond=rows_per_cond, tm=tm)
    widths = (3 * HY_W, FN_W, NA_W, NA_W, NA_W)
    dts = (BF16, BF16, BF16, F32, F32)
    return pl.pallas_call(
        kern,
        grid=(n // tm,),
        in_specs=[pl.BlockSpec((tm, D), lambda i: (i, 0)),
                  pl.BlockSpec((None, 8, N_MOD * D), lambda i: (l, 0, 0)),
                  pl.BlockSpec((None, 1, D), lambda i: (l, 0, 0)),
                  pl.BlockSpec((None, D, BR_W), lambda i: (l, 0, 0))],
        out_specs=[pl.BlockSpec((tm, w), lambda i: (i, 0)) for w in widths],
        out_shape=[jax.ShapeDtypeStruct((n, w), dt) for w, dt in zip(widths, dts)],
        compiler_params=_cp(("parallel",)),
        name="in_proj",
    )(x, mod, norm_mix, w_in_b)


def _cs_tables(L, N):
    k = np.arange(L, dtype=np.int64)[:, None]
    na = np.arange(0, L, 64, dtype=np.int64)[None, :]
    nb = np.arange(64, dtype=np.int64)[None, :]
    ang_a = 2.0 * np.pi * ((k * na) % N).astype(np.float64) / N
    ang_b = 2.0 * np.pi * ((k * nb) % N).astype(np.float64) / N
    ca = jnp.asarray(np.cos(ang_a), F32)[:, :, None]
    sa = jnp.asarray(np.sin(ang_a), F32)[:, :, None]
    cb = jnp.asarray(np.cos(ang_b), F32)[:, None, :]
    sb = jnp.asarray(np.sin(ang_b), F32)[:, None, :]
    cos = (ca * cb - sa * sb).reshape(L, L)
    sin = (sa * cb + ca * sb).reshape(L, L)
    return cos, sin


def _hyena_tables(L):
    cos, sin = _cs_tables(L, 2 * L)
    alt = (1 - 2 * (jnp.arange(L) % 2)).astype(F32)
    row0 = (jnp.arange(L) == 0)
    s_fwd = jnp.where(row0[:, None], alt[None, :], sin)
    s_inv = jnp.where(row0[None, :], alt[:, None], sin)
    return cos.astype(BF16), s_fwd.astype(BF16), s_inv.astype(BF16)


def _fnet_tables(L):
    cos, sin = _cs_tables(L, L)
    g = FN_GROUP_DIM
    kk = np.arange(g)[:, None] * np.arange(g)[None, :]
    ang = 2.0 * np.pi * (kk % g) / g
    scale = 1.0 / math.sqrt(L * g)
    cb = np.kron(np.eye(FN_GROUPS), np.cos(ang)) * scale
    sb = np.kron(np.eye(FN_GROUPS), np.sin(ang)) * scale
    csb = jnp.asarray(np.concatenate([cb, sb], axis=1), F32).astype(BF16)
    return cos.astype(BF16), sin.astype(BF16), csb


def _hy_filt_a_kernel(z_ref, w1_ref, b1_ref, w2_ref, b2_ref, w3a_ref, w3b_ref, b3a_ref, b3b_ref,
                      fr_ref, dec_ref, fp_ref, fm_ref, last_ref, nyq_ref, pad_ref, a_ref, *, L):
    z = z_ref[...]

    @pl.when(pl.program_id(1) == 0)
    def _():
        fr = fr_ref[...]
        a1 = jnp.sin(fr * (_dot_hi(z, w1_ref[...]) + b1_ref[...]))
        a_ref[...] = jnp.sin(fr * (_dot_hi(a1, w2_ref[...]) + b2_ref[...]))

    a = a_ref[...]
    window = jnp.exp(-z[:, 0:1] * jnp.abs(dec_ref[...]))
    f0 = (_dot_hi(a, w3a_ref[...]) + b3a_ref[...]) * window
    f1 = (_dot_hi(a, w3b_ref[...]) + b3b_ref[...]) * window
    norm = (jnp.sum(jnp.abs(f0), axis=0, keepdims=True)
            + jnp.sum(jnp.abs(f1), axis=0, keepdims=True) + 1e-6)
    f0 = f0 / norm
    f1 = f1 / norm
    pad_ref[pl.ds(0, 8), :] = jnp.zeros((8, 128), F32)
    pad_ref[pl.ds(8, L), :] = f1
    f1s = pad_ref[pl.ds(7, L), :]
    last = pad_ref[pl.ds(L + 7, 1), :]
    fp = f0 + f1s
    fp_ref[...] = fp.astype(BF16)
    fm_ref[...] = (f0 - f1s).astype(BF16)
    last_ref[...] = last
    row = lax.broadcasted_iota(I32, (L, 128), 0)
    alt = (1 - 2 * (row & 1)).astype(F32)
    nyq_ref[...] = jnp.sum(alt * fp, axis=0, keepdims=True) + last


def _hy_filt_b_kernel(c_ref, s_ref, fp_ref, fm_ref, last_ref, nyq_ref, hc_ref, hs_ref, *, tk):
    kt = pl.program_id(0)
    row = kt * tk + lax.broadcasted_iota(I32, (tk, HY_W), 0)
    alt = (1 - 2 * (row & 1)).astype(F32)
    hc_ref[...] = _dot(c_ref[...], fp_ref[...]) + alt * last_ref[...]
    hs = _dot(s_ref[...], fm_ref[...])
    hs_ref[...] = jnp.where(row == 0, nyq_ref[...], hs)


def _hyena_filters(L, tabs, p):
    cos, s_fwd, _ = tabs
    t = np.linspace(0.0, 1.0, L, dtype=np.float32)[:, None]
    bands = (HY_EMB - 1) // 2
    omega = (2.0 * math.pi * np.arange(L, dtype=np.float32)[:, None] / L).astype(np.float32)
    f = np.linspace(1e-4, bands - 1, bands, dtype=np.float32)[None, :]
    z = np.concatenate([t, np.cos(f * omega), -np.sin(f * omega)], axis=-1).astype(np.float32)
    z = jnp.asarray(np.pad(z, ((0, 0), (0, HY_EMB_PAD - HY_EMB))))
    w1 = jnp.pad(p["hy_ffn_w1"], ((0, 0), (0, HY_EMB_PAD - HY_EMB), (0, 0)))
    r3 = lambda a: a.reshape(DEPTH, 1, a.shape[-1])
    nch = HY_W // 128
    vec = lambda: pl.BlockSpec((None, 1, HY_FF), lambda l, c: (l, 0, 0))
    fp, fm, last, nyq = pl.pallas_call(
        functools.partial(_hy_filt_a_kernel, L=L),
        grid=(DEPTH, nch),
        in_specs=[pl.BlockSpec((L, HY_EMB_PAD), lambda l, c: (0, 0)),
                  pl.BlockSpec((None, HY_EMB_PAD, HY_FF), lambda l, c: (l, 0, 0)), vec(),
                  pl.BlockSpec((None, HY_FF, HY_FF), lambda l, c: (l, 0, 0)), vec(),
                  pl.BlockSpec((None, HY_FF, 128), lambda l, c: (l, 0, c)),
                  pl.BlockSpec((None, HY_FF, 128), lambda l, c: (l, 0, nch + c)),
                  pl.BlockSpec((None, 1, 128), lambda l, c: (l, 0, c)),
                  pl.BlockSpec((None, 1, 128), lambda l, c: (l, 0, nch + c)),
                  vec(),
                  pl.BlockSpec((None, 1, 128), lambda l, c: (l, 0, c))],
        out_specs=[pl.BlockSpec((None, L, 128), lambda l, c: (l, 0, c)),
                   pl.BlockSpec((None, L, 128), lambda l, c: (l, 0, c)),
                   pl.BlockSpec((None, 1, 128), lambda l, c: (l, 0, c)),
                   pl.BlockSpec((None, 1, 128), lambda l, c: (l, 0, c))],
        out_shape=[jax.ShapeDtypeStruct((DEPTH, L, HY_W), BF16),
                   jax.ShapeDtypeStruct((DEPTH, L, HY_W), BF16),
                   jax.ShapeDtypeStruct((DEPTH, 1, HY_W), F32),
                   jax.ShapeDtypeStruct((DEPTH, 1, HY_W), F32)],
        scratch_shapes=[pltpu.VMEM((L + 8, 128), F32), pltpu.VMEM((L, HY_FF), F32)],
        compiler_params=_cp(("parallel", "arbitrary")),
        name="hyena_filter_taps",
    )(z, w1, r3(p["hy_ffn_b1"]), p["hy_ffn_w2"], r3(p["hy_ffn_b2"]), p["hy_ffn_w3"], p["hy_ffn_w3"],
      r3(p["hy_ffn_b3"]), r3(p["hy_ffn_b3"]), r3(p["hy_sin_freq"]), r3(p["hy_decay"]))
    tk = min(DFT_TILE, L)
    full = lambda: pl.BlockSpec((None, L, HY_W), lambda kt, l: (l, 0, 0))
    one = lambda: pl.BlockSpec((None, 1, HY_W), lambda kt, l: (l, 0, 0))
    hc, hs = pl.pallas_call(
        functools.partial(_hy_filt_b_kernel, tk=tk),
        grid=(L // tk, DEPTH),
        in_specs=[pl.BlockSpec((tk, L), lambda kt, l: (kt, 0)),
                  pl.BlockSpec((tk, L), lambda kt, l: (kt, 0)),
                  full(), full(), one(), one()],
        out_specs=[pl.BlockSpec((None, tk, HY_W), lambda kt, l: (l, kt, 0)),
                   pl.BlockSpec((None, tk, HY_W), lambda kt, l: (l, kt, 0))],
        out_shape=[jax.ShapeDtypeStruct((DEPTH, L, HY_W), F32)] * 2,
        compiler_params=_cp(("parallel", "parallel")),
        name="hyena_filter_dft",
    )(cos, s_fwd, fp, fm, last, nyq)
    return hc, hs


def _hy_pre_kernel(u0_ref, u1_ref, u2_ref, w0_ref, w1_ref, w2_ref, b0_ref, b1_ref, b2_ref,
                   zz_ref, x0_ref, pad_ref, *, L):
    zeros8 = jnp.zeros((8, 128), F32)
    pad_ref[pl.ds(0, 8), :] = zeros8
    pad_ref[pl.ds(L + 8, 8), :] = zeros8

    def conv(u_ref, w_ref, b_ref):
        pad_ref[pl.ds(8, L), :] = u_ref[...].astype(F32)
        w = w_ref[...]
        return (w[0:1] * pad_ref[pl.ds(7, L), :] + w[1:2] * pad_ref[pl.ds(8, L), :]
                + w[2:3] * pad_ref[pl.ds(9, L), :] + b_ref[...])

    x0_ref[...] = conv(u0_ref, w0_ref, b0_ref).astype(BF16)
    x1 = conv(u1_ref, w1_ref, b1_ref)
    v = conv(u2_ref, w2_ref, b2_ref)
    zz_ref[...] = (x1 * v).astype(BF16)


def _hy_fwd_kernel(c_ref, s_ref, zza_ref, zzb_ref, hc_ref, hs_ref, pra_ref, pia_ref, prb_ref, pib_ref,
                   *, tk, L):
    kt = pl.program_id(0)
    zz = jnp.concatenate([zza_ref[...], zzb_ref[...]], axis=1)
    zc2 = _dot(c_ref[...], zz)
    zs2 = _dot(s_ref[...], zz)
    hc = hc_ref[...]
    hs = hs_ref[...]
    row = kt * tk + lax.broadcasted_iota(I32, (tk, HY_W), 0)
    first = row == 0
    wgt = jnp.where(first, 1.0 / (2 * L), 2.0 / (2 * L)).astype(F32)
    for i, (pr_ref, pi_ref) in enumerate(((pra_ref, pia_ref), (prb_ref, pib_ref))):
        zc = zc2[:, i * HY_W:(i + 1) * HY_W]
        zs = zs2[:, i * HY_W:(i + 1) * HY_W]
        ss = zs * hs
        pr_ref[...] = ((zc * hc - jnp.where(first, 0.0, ss)) * wgt).astype(BF16)
        pi_ref[...] = (jnp.where(first, ss, zc * hs + zs * hc) * wgt).astype(BF16)


def _hy_inv_kernel(c_ref, st_ref, pra_ref, pia_ref, prb_ref, pib_ref, zza_ref, x0a_ref, zzb_ref, x0b_ref,
                   skip_ref, oa_ref, ob_ref):
    pr = jnp.concatenate([pra_ref[...], prb_ref[...]], axis=1)
    pi = jnp.concatenate([pia_ref[...], pib_ref[...]], axis=1)
    y2 = _dot(c_ref[...], pr) + _dot(st_ref[...], pi)
    for i, (zz_ref, x0_ref, o_ref) in enumerate(((zza_ref, x0a_ref, oa_ref), (zzb_ref, x0b_ref, ob_ref))):
        zz = zz_ref[...].astype(F32)
        y = y2[:, i * HY_W:(i + 1) * HY_W]
        o_ref[...] = (x0_ref[...].astype(F32) * (y + zz * skip_ref[...])).astype(BF16)


def _hyena_call(l, u_hy, B, L, tabs, hc, hs, p):
    cos, s_fwd, s_inv = tabs
    n = B * L
    nch = HY_W // 128
    cw = p["hy_conv_w"].reshape(DEPTH, 3, 3 * HY_W)
    cb = p["hy_conv_b"].reshape(DEPTH, 1, 3 * HY_W)
    ub = lambda part: pl.BlockSpec((L, 128), lambda b, c: (b, part * nch + c))
    wb = lambda part: pl.BlockSpec((None, 3, 128), lambda b, c: (l, 0, part * nch + c))
    bb = lambda part: pl.BlockSpec((None, 1, 128), lambda b, c: (l, 0, part * nch + c))
    zz, x0 = pl.pallas_call(
        functools.partial(_hy_pre_kernel, L=L),
        grid=(B, nch),
        in_specs=[ub(0), ub(1), ub(2), wb(0), wb(1), wb(2), bb(0), bb(1), bb(2)],
        out_specs=[pl.BlockSpec((L, 128), lambda b, c: (b, c))] * 2,
        out_shape=[jax.ShapeDtypeStruct((n, HY_W), BF16)] * 2,
        scratch_shapes=[pltpu.VMEM((L + 16, 128), F32)],
        compiler_params=_cp(("parallel", "parallel")),
        name="hyena_short_conv",
    )(u_hy, u_hy, u_hy, cw, cw, cw, cb, cb, cb)

    tk = min(DFT_TILE, L)
    nk = L // tk
    hb = B // 2
    tab = lambda: pl.BlockSpec((tk, L), lambda kt, b: (kt, 0))
    seq = lambda off: pl.BlockSpec((L, HY_W), lambda kt, b: (b + off, 0))
    til = lambda off: pl.BlockSpec((tk, HY_W), lambda kt, b: ((b + off) * nk + kt, 0))
    flt = lambda: pl.BlockSpec((None, tk, HY_W), lambda kt, b: (l, kt, 0))
    half = jax.ShapeDtypeStruct((hb * L, HY_W), BF16)
    pr_a, pi_a, pr_b, pi_b = pl.pallas_call(
        functools.partial(_hy_fwd_kernel, tk=tk, L=L),
        grid=(nk, hb),
        in_specs=[tab(), tab(), seq(0), seq(hb), flt(), flt()],
        out_specs=[til(0)] * 4,
        out_shape=[half] * 4,
        compiler_params=_cp(("parallel", "parallel")),
        name="hyena_dft_fwd",
    )(cos, s_fwd, zz, zz, hc, hs)

    y_a, y_b = pl.pallas_call(
        _hy_inv_kernel,
        grid=(nk, hb),
        in_specs=[tab(), tab(), seq(0), seq(0), seq(0), seq(0),
                  til(0), til(0), til(hb), til(hb),
                  pl.BlockSpec((None, 1, HY_W), lambda qt, b: (l, 0, 0))],
        out_specs=[til(0)] * 2,
        out_shape=[half] * 2,
        compiler_params=_cp(("parallel", "parallel")),
        name="hyena_dft_inv",
    )(cos, s_inv, pr_a, pi_a, pr_b, pi_b, zz, x0, zz, x0, p["hy_skip"].reshape(DEPTH, 1, HY_W))
    return y_a, y_b


def _fnet_kernel(u_ref, csb_ref, c_ref, s_ref, o_ref, uc_ref, us_ref):
    @pl.when(pl.program_id(1) == 0)
    def _():
        t = _dot(u_ref[...], csb_ref[...])
        uc_ref[...] = t[:, :FN_W].astype(BF16)
        us_ref[...] = t[:, FN_W:].astype(BF16)

    o_ref[...] = (_dot(c_ref[...], uc_ref[...]) - _dot(s_ref[...], us_ref[...])).astype(BF16)


def _fnet_call(u_fn, B, L, tabs):
    cos, sin, csb = tabs
    tq = min(DFT_TILE, L)
    nq = L // tq
    return pl.pallas_call(
        _fnet_kernel,
        grid=(B, nq),
        in_specs=[pl.BlockSpec((L, FN_W), lambda b, qt: (b, 0)),
                  pl.BlockSpec((FN_W, 2 * FN_W), lambda b, qt: (0, 0)),
                  pl.BlockSpec((tq, L), lambda b, qt: (qt, 0)),
                  pl.BlockSpec((tq, L), lambda b, qt: (qt, 0))],
        out_specs=pl.BlockSpec((tq, FN_W), lambda b, qt: (b * nq + qt, 0)),
        out_shape=jax.ShapeDtypeStruct((B * L, FN_W), BF16),
        scratch_shapes=[pltpu.VMEM((L, FN_W), BF16), pltpu.VMEM((L, FN_W), BF16)],
        compiler_params=_cp(("parallel", "arbitrary")),
        name="fnet_dft2",
    )(u_fn, csb, cos, sin)


def _ctx_attn_kernel(q_ref, k_ref, v_ref, o_ref):
    for h in range(NA_H):
        sl = slice(h * NA_D, (h + 1) * NA_D)
        q = q_ref[:, sl]
        k = k_ref[:, sl].astype(BF16)
        v = v_ref[:, sl].astype(BF16)
        s = _dot_nt(q, k) * ATTN_SCALE
        m = jnp.max(s, axis=-1, keepdims=True)
        e = jnp.exp(s - m)
        den = jnp.sum(e, axis=-1, keepdims=True)
        o_ref[:, sl] = (_dot(e.astype(BF16), v) / den).astype(BF16)


def _ctx_attn_call(q, k, v, B, L):
    blk = lambda: pl.BlockSpec((L, NA_W), lambda b: (b, 0))
    return pl.pallas_call(
        _ctx_attn_kernel,
        grid=(B,),
        in_specs=[blk(), blk(), blk()],
        out_specs=blk(),
        out_shape=jax.ShapeDtypeStruct((B * L, NA_W), BF16),
        compiler_params=_cp(("parallel",)),
        name="context_attention",
    )(q, k, v)


def _nbr_attn_kernel(q_ref, k_ref, v_ref, kc_ref, vc_ref, bias_ref, o_ref, *, rows, band_rows):
    r0 = pl.program_id(1) * NBR_GROUP
    bs = jnp.clip(r0 - WIN_R // 2, 0, rows - band_rows)
    start = pl.multiple_of(bs * GRID_W, GRID_W)
    band = band_rows * GRID_W
    for h in range(NA_H):
        sl = slice(h * NA_D, (h + 1) * NA_D)
        q = q_ref[:, sl]
        kb = k_ref[pl.ds(start, band), sl].astype(BF16)
        vb = v_ref[pl.ds(start, band), sl].astype(BF16)
        kc = kc_ref[:, sl].astype(BF16)
        vc = vc_ref[:, sl].astype(BF16)
        s_loc = _dot_nt(q, kb) * ATTN_SCALE + bias_ref[h]
        s_ctx = _dot_nt(q, kc) * ATTN_SCALE
        m = jnp.maximum(jnp.max(s_loc, axis=-1, keepdims=True), jnp.max(s_ctx, axis=-1, keepdims=True))
        e_loc = jnp.exp(s_loc - m)
        e_ctx = jnp.exp(s_ctx - m)
        den = jnp.sum(e_loc, axis=-1, keepdims=True) + jnp.sum(e_ctx, axis=-1, keepdims=True)
        o = _dot(e_loc.astype(BF16), vb) + _dot(e_ctx.astype(BF16), vc)
        o_ref[:, sl] = (o / den).astype(BF16)


def _nbr_bias_table(rel_bias, rows):
    wr = min(WIN_R, rows)
    band_rows = min(rows, wr + NBR_GROUP - 1)
    n_groups = rows // NBR_GROUP
    cols = np.arange(GRID_W)
    cs = np.clip(cols - WIN_C // 2, 0, GRID_W - WIN_C)
    mask = (cols[None, :] >= cs[:, None]) & (cols[None, :] < cs[:, None] + WIN_C)
    dc = np.clip(cols[None, :] - cols[:, None], -(WIN_C - 1), WIN_C - 1) + WIN_C - 1
    onehot = ((dc[None] == np.arange(2 * WIN_C - 1)[:, None, None]) & mask[None]).astype(np.float32)
    t = jnp.einsum("lhdj,jqk->lhdqk", rel_bias.astype(F32), jnp.asarray(onehot),
                   precision=lax.Precision.HIGHEST)
    t = t + jnp.asarray(np.where(mask, 0.0, MASK_NEG).astype(np.float32))
    masked = jnp.full(t.shape[:2] + t.shape[3:], MASK_NEG, F32)
    kinds = []
    for g in sorted({0, min(1, n_groups - 1), n_groups - 1}):
        r0 = g * NBR_GROUP
        bs = int(np.clip(r0 - wr // 2, 0, rows - band_rows))
        q_rows = []
        for i in range(NBR_GROUP):
            r = r0 + i
            rs = int(np.clip(r - wr // 2, 0, rows - wr))
            blocks = []
            for u in range(band_rows):
                key_row = bs + u
                blocks.append(t[:, :, key_row - r + WIN_R - 1] if rs <= key_row < rs + wr else masked)
            q_rows.append(jnp.concatenate(blocks, axis=-1))
        kinds.append(jnp.concatenate(q_rows, axis=-2))
    while len(kinds) < 3:
        kinds.append(kinds[-1])
    return jnp.stack(kinds, axis=1), band_rows


def _nbr_attn_call(l, q, k, v, ck, cv, bias, band_rows, B, L):
    rows = L // GRID_W
    n_groups = rows // NBR_GROUP
    P = ck.shape[2]
    gq = NBR_GROUP * GRID_W

    def kind(g):
        return jnp.where(g == 0, 0, jnp.where(g == n_groups - 1, 2, 1))

    return pl.pallas_call(
        functools.partial(_nbr_attn_kernel, rows=rows, band_rows=band_rows),
        grid=(B, n_groups),
        in_specs=[pl.BlockSpec((gq, NA_W), lambda b, g: (b * n_groups + g, 0)),
                  pl.BlockSpec((L, NA_W), lambda b, g: (b, 0)),
                  pl.BlockSpec((L, NA_W), lambda b, g: (b, 0)),
                  pl.BlockSpec((None, None, P, NA_W), lambda b, g: (b, l, 0, 0)),
                  pl.BlockSpec((None, None, P, NA_W), lambda b, g: (b, l, 0, 0)),
                  pl.BlockSpec((None, None, NA_H, gq, band_rows * GRID_W),
                               lambda b, g: (l, kind(g), 0, 0, 0))],
        out_specs=pl.BlockSpec((gq, NA_W), lambda b, g: (b * n_groups + g, 0)),
        out_shape=jax.ShapeDtypeStruct((B * L, NA_W), BF16),
        compiler_params=_cp(("parallel", "parallel")),
        name="neighbourhood_attention",
    )(q, k, v, ck, cv, bias)


def _merge_kernel(x_ref, yhya_ref, yhyb_ref, yfn_ref, yna_ref, mod_ref, gmix_ref, gffn_ref, wg_ref, why_ref,
                  wfn_ref, wna_ref, wout_ref, rw_ref, rb_ref, cnt0_ref,
                  xn_ref, h2_ref, idx_ref, gate_ref, rank_ref, cnt_ref, run_ref, *,
                  cond_base, rows_per_cond, tm, half_tiles):
    i = pl.program_id(0)
    cnd = cond_base + (i * tm) // rows_per_cond

    @pl.when(i == 0)
    def _():
        run_ref[...] = cnt0_ref[...]

    y_hy = jnp.where(i < half_tiles, yhya_ref[...], yhyb_ref[...])

    x = x_ref[...]
    h = (_rms(x, gmix_ref[...]) * (1.0 + _mod_row(mod_ref, cnd, 1)) + _mod_row(mod_ref, cnd, 0)).astype(BF16)
    merged = None
    for b, (y_b, w_ref) in enumerate(((y_hy, why_ref), (yfn_ref[...], wfn_ref), (yna_ref[...], wna_ref))):
        gate_b = jax.nn.sigmoid(_dot(h, wg_ref[:, b * D:(b + 1) * D]))
        term = gate_b * _dot(y_b, w_ref[...])
        merged = term if merged is None else merged + term
    xn = x + _mod_row(mod_ref, cnd, 2) * _dot(merged.astype(BF16), wout_ref[...])
    xn_ref[...] = xn
    h2 = _rms(xn, gffn_ref[...]) * (1.0 + _mod_row(mod_ref, cnd, 4)) + _mod_row(mod_ref, cnd, 3)
    for cc in range(D // 128):
        h2_ref[pl.ds(cc, tm, stride=8), :] = h2[:, cc * 128:(cc + 1) * 128]

    h2_hi = h2.astype(BF16)
    h2_lo = (h2 - h2_hi.astype(F32)).astype(BF16)
    rw = rw_ref[...]
    rw_hi = rw.astype(BF16)
    rw_lo = (rw - rw_hi.astype(F32)).astype(BF16)
    logits = _dot(h2_hi, rw_hi) + _dot(h2_lo, rw_hi) + _dot(h2_hi, rw_lo) + rb_ref[...]
    lane = lax.broadcasted_iota(I32, (tm, N_EXP), 1)
    lane4 = lax.broadcasted_iota(I32, (tm, TOP_K), 1)
    rem = logits
    vals, idxs = [], []
    for _ in range(TOP_K):
        m = jnp.max(rem, axis=-1, keepdims=True)
        ix = jnp.min(jnp.where(rem == m, lane, N_EXP), axis=-1, keepdims=True)
        vals.append(m)
        idxs.append(ix)
        rem = jnp.where(lane == ix, -jnp.inf, rem)
    es = [jnp.exp(v - vals[0]) for v in vals]
    den = es[0] + es[1] + es[2] + es[3]

    onehot = jnp.zeros((tm, N_EXP), F32)
    for ix in idxs:
        onehot = onehot + (lane == ix).astype(F32)
    r_i = lax.broadcasted_iota(I32, (tm, tm), 0)
    c_i = lax.broadcasted_iota(I32, (tm, tm), 1)
    lower = (c_i < r_i).astype(BF16)
    before = _dot(lower, onehot.astype(BF16)) + run_ref[...]

    idx_o = jnp.zeros((tm, TOP_K), I32)
    gate_o = jnp.zeros((tm, TOP_K), F32)
    rank_o = jnp.zeros((tm, TOP_K), F32)
    for kk in range(TOP_K):
        rk = jnp.sum(jnp.where(lane == idxs[kk], before, 0.0), axis=-1, keepdims=True)
        idx_o = jnp.where(lane4 == kk, idxs[kk], idx_o)
        gate_o = jnp.where(lane4 == kk, es[kk] / den, gate_o)
        rank_o = jnp.where(lane4 == kk, rk, rank_o)
    idx_ref[...] = idx_o
    gate_ref[...] = gate_o
    rank_ref[...] = rank_o.astype(I32)
    run_ref[...] = run_ref[...] + jnp.sum(onehot, axis=0, keepdims=True)
    cnt_ref[...] = run_ref[...]


def _merge_call(l, x, y_hy, y_fn, y_na, mod, wts, cnt0, cond_base, rows_per_cond):
    n = x.shape[0]
    tm = MERGE_TILE
    y_hy_a, y_hy_b = y_hy
    half_tiles = y_hy_a.shape[0] // tm
    assert y_hy_a.shape[0] % tm == 0
    kern = functools.partial(_merge_kernel, cond_base=cond_base, rows_per_cond=rows_per_cond, tm=tm,
                             half_tiles=half_tiles)
    row = lambda w: pl.BlockSpec((tm, w), lambda i: (i, 0))
    lw = lambda a, b: pl.BlockSpec((None, a, b), lambda i: (l, 0, 0), pipeline_mode=pl.Buffered(1))
    return pl.pallas_call(
        kern,
        grid=(n // tm,),
        in_specs=[row(D),
                  pl.BlockSpec((tm, HY_W), lambda i: (jnp.minimum(i, half_tiles - 1), 0)),
                  pl.BlockSpec((tm, HY_W), lambda i: (jnp.maximum(i - half_tiles, 0), 0)),
                  row(FN_W), row(NA_W),
                  lw(8, N_MOD * D), lw(1, D), lw(1, D),
                  lw(D, 3 * D), lw(HY_W, D), lw(FN_W, D), lw(NA_W, D), lw(D, D),
                  lw(D, N_EXP), lw(1, N_EXP),
                  pl.BlockSpec((1, N_EXP), lambda i: (0, 0))],
        out_specs=[row(D), pl.BlockSpec((tm * 8, 128), lambda i: (i, 0)), row(TOP_K), row(TOP_K), row(TOP_K),
                   pl.BlockSpec((1, N_EXP), lambda i: (0, 0))],
        out_shape=[jax.ShapeDtypeStruct((n, D), F32), jax.ShapeDtypeStruct((n * 8, 128), F32),
                   jax.ShapeDtypeStruct((n, TOP_K), I32), jax.ShapeDtypeStruct((n, TOP_K), F32),
                   jax.ShapeDtypeStruct((n, TOP_K), I32), jax.ShapeDtypeStruct((1, N_EXP), F32)],
        scratch_shapes=[pltpu.VMEM((1, N_EXP), F32)],
        compiler_params=_cp(("arbitrary",)),
        name="merge_router",
    )(x, y_hy_a, y_hy_b, y_fn, y_na, mod, wts["norm_mix"], wts["norm_ffn"], wts["w_g"], wts["w_hy_out"],
      wts["w_fn_out"], wts["w_na_out"], wts["w_out"], wts["router_w"], wts["router_b"], cnt0)


def _dispatch_kernel(slot_ref, h_ref, xs_in_ref, xs_ref, sem, *, tm):
    del xs_in_ref

    def issue(t, carry):
        src = h_ref.at[pl.ds(pl.multiple_of(t * 8, 8), 8)]
        for kk in range(TOP_K):
            s = slot_ref[0, 0, t * TOP_K + kk]
            dst = xs_ref.at[pl.ds(pl.multiple_of(s * 8, 8), 8)]
            pltpu.make_async_copy(src, dst, sem).start(priority=kk % 2)
        return carry

    lax.fori_loop(0, tm, issue, 0)
    for kk in range(TOP_K):
        pltpu.make_async_copy(h_ref, xs_ref.at[pl.ds(0, tm * 8)], sem).wait()


def _dispatch_call(slot, h2, xs):
    n = h2.shape[0] // 8
    tm = ROW_TILE
    slot3 = slot.reshape(n // tm, 1, tm * TOP_K)
    return pl.pallas_call(
        functools.partial(_dispatch_kernel, tm=tm),
        grid=(n // tm,),
        in_specs=[pl.BlockSpec((1, 1, tm * TOP_K), lambda i: (i, 0, 0), memory_space=pltpu.SMEM),
                  pl.BlockSpec((tm * 8, 128), lambda i: (i, 0)),
                  pl.BlockSpec(memory_space=pl.ANY)],
        out_specs=pl.BlockSpec(memory_space=pl.ANY),
        out_shape=jax.ShapeDtypeStruct(xs.shape, xs.dtype),
        scratch_shapes=[pltpu.SemaphoreType.DMA(())],
        input_output_aliases={2: 0},
        compiler_params=_cp(("arbitrary",)),
        name="moe_dispatch",
    )(slot3, h2, xs)


def _moe_kernel(be_ref, nu_ref, x_ref, w1_ref, b1g_ref, b1l_ref, w2_ref, b2_ref, y_ref,
                w1g_s, w1l_s, w2_s):
    j = pl.program_id(0)
    live = j < nu_ref[0]
    new_expert = jnp.logical_or(j == 0, be_ref[j] != be_ref[jnp.maximum(j - 1, 0)])

    @pl.when(jnp.logical_and(live, new_expert))
    def _():
        grp = 2 * DEINT
        r = lax.broadcasted_iota(I32, (grp, grp), 0)
        c = lax.broadcasted_iota(I32, (grp, grp), 1)
        perm = (r == jnp.where(c < DEINT, 2 * c, 2 * (c - DEINT) + 1)).astype(BF16)
        for ch in range(2 * FF // grp):
            t = _dot(w1_ref[:, ch * grp:(ch + 1) * grp].astype(BF16), perm).astype(BF16)
            w1g_s[:, ch * DEINT:(ch + 1) * DEINT] = t[:, :DEINT]
            w1l_s[:, ch * DEINT:(ch + 1) * DEINT] = t[:, DEINT:]
        w2_s[...] = w2_ref[...].astype(BF16)

    @pl.when(live)
    def _():
        x = jnp.concatenate([x_ref[pl.ds(cc, MOE_ROWS, stride=8), :] for cc in range(D // 128)],
                            axis=1).astype(BF16)
        g = jnp.minimum(_dot(x, w1g_s[...]) + b1g_ref[...], SWIGLU_LIMIT)
        lin = jnp.clip(_dot(x, w1l_s[...]) + b1l_ref[...], -SWIGLU_LIMIT, SWIGLU_LIMIT)
        act = g * jax.nn.sigmoid(SWIGLU_ALPHA * g) * (lin + 1.0)
        y = _dot(act.astype(BF16), w2_s[...]) + b2_ref[...]
        for cc in range(D // 128):
            y_ref[pl.ds(cc, MOE_ROWS, stride=8), :] = y[:, cc * 128:(cc + 1) * 128]

    @pl.when(jnp.logical_not(live))
    def _():
        y_ref[...] = jnp.zeros_like(y_ref)


def _moe_call(l, xs, block_expert, n_used, wts):
    n_slots = xs.shape[0] // 8
    R = MOE_ROWS
    bspec = lambda: pl.BlockSpec((None, None, 1, FF), lambda j, be, nu: (l, be[j], 0, 0))
    grid_spec = pltpu.PrefetchScalarGridSpec(
        num_scalar_prefetch=2,
        grid=(n_slots // R,),
        in_specs=[pl.BlockSpec((R * 8, 128), lambda j, be, nu: (jnp.minimum(j, nu[0] - 1), 0)),
                  pl.BlockSpec((None, None, D, 2 * FF), lambda j, be, nu: (l, be[j], 0, 0)),
                  bspec(), bspec(),
                  pl.BlockSpec((None, None, FF, D), lambda j, be, nu: (l, be[j], 0, 0)),
                  bspec()],
        out_specs=pl.BlockSpec((R * 8, 128), lambda j, be, nu: (j, 0)),
        scratch_shapes=[pltpu.VMEM((D, FF), BF16), pltpu.VMEM((D, FF), BF16),
                        pltpu.VMEM((FF, D), BF16)],
    )
    return pl.pallas_call(
        _moe_kernel,
        grid_spec=grid_spec,
        out_shape=jax.ShapeDtypeStruct((n_slots * 8, 128), F32),
        compiler_params=_cp(("arbitrary",)),
        name="moe_experts",
    )(block_expert, n_used, xs, wts["w1"], wts["b1g"], wts["b1l"], wts["w2"], wts["b2"])


def _combine_kernel(slot_ref, slot_next_ref, xn_ref, gate_ref, mod_ref, fin_ref, ys_ref, o_ref, ybuf, sem, *,
                    cond_base, rows_per_cond, tm, final):
    i = pl.program_id(0)
    n_steps = pl.num_programs(0)
    cnd = cond_base + (i * tm) // rows_per_cond

    def issue(s_ref, buf):
        def body(t, carry):
            for kk in range(TOP_K):
                s = s_ref[0, 0, t * TOP_K + kk]
                src = ys_ref.at[pl.ds(pl.multiple_of(s * 8, 8), 8)]
                dst = ybuf.at[buf, kk, pl.ds(pl.multiple_of(t * 8, 8), 8)]
                pltpu.make_async_copy(src, dst, sem.at[buf]).start(priority=kk % 2)
            return carry

        lax.fori_loop(0, tm, body, 0)

    def finish(buf):
        for kk in range(TOP_K):
            pltpu.make_async_copy(ys_ref.at[pl.ds(0, tm * 8)], ybuf.at[buf, kk], sem.at[buf]).wait()
        gate = gate_ref[...]
        g2 = _mod_row(mod_ref, cnd, 5)
        sumsq = jnp.zeros((tm, 1), F32)
        for cc in range(D // 128):
            cols = slice(cc * 128, (cc + 1) * 128)
            part = gate[:, 0:1] * ybuf[buf, 0, pl.ds(cc, tm, stride=8), :]
            for kk in range(1, TOP_K):
                part = part + gate[:, kk:kk + 1] * ybuf[buf, kk, pl.ds(cc, tm, stride=8), :]
            out_c = xn_ref[:, cols] + g2[:, cols] * part
            o_ref[:, cols] = out_c
            if final:
                sumsq = sumsq + jnp.sum(out_c * out_c, axis=-1, keepdims=True)
        if final:
            o_ref[...] = o_ref[...] * lax.rsqrt(sumsq * (1.0 / D) + RMS_EPS) * fin_ref[...]

    @pl.when(i == 0)
    def _():
        issue(slot_ref, 0)

    for buf in range(2):
        @pl.when(jnp.logical_and(i % 2 == buf, i + 1 < n_steps))
        def _():
            issue(slot_next_ref, 1 - buf)

        @pl.when(i % 2 == buf)
        def _():
            finish(buf)


def _combine_call(l, slot, xn, gate, mod, final_norm, ys, cond_base, rows_per_cond, final):
    n = xn.shape[0]
    tm = ROW_TILE
    nt = n // tm
    slot3 = slot.reshape(nt, 1, tm * TOP_K)
    kern = functools.partial(_combine_kernel, cond_base=cond_base, rows_per_cond=rows_per_cond,
                             tm=tm, final=final)
    return pl.pallas_call(
        kern,
        grid=(nt,),
        in_specs=[pl.BlockSpec((1, 1, tm * TOP_K), lambda i: (i, 0, 0), memory_space=pltpu.SMEM),
                  pl.BlockSpec((1, 1, tm * TOP_K), lambda i: (jnp.minimum(i + 1, nt - 1), 0, 0),
                               memory_space=pltpu.SMEM),
                  pl.BlockSpec((tm, D), lambda i: (i, 0)),
                  pl.BlockSpec((tm, TOP_K), lambda i: (i, 0)),
                  pl.BlockSpec((None, 8, N_MOD * D), lambda i: (l, 0, 0)),
                  pl.BlockSpec((1, D), lambda i: (0, 0)),
                  pl.BlockSpec(memory_space=pl.ANY)],
        out_specs=pl.BlockSpec((tm, D), lambda i: (i, 0)),
        out_shape=jax.ShapeDtypeStruct((n, D), F32),
        scratch_shapes=[pltpu.VMEM((2, TOP_K, tm * 8, 128), F32), pltpu.SemaphoreType.DMA((2,))],
        compiler_params=_cp(("arbitrary",)),
        name="moe_combine",
    )(slot3, slot3, xn, gate, mod, final_norm.reshape(1, D), ys)


def kernel(x_prompt, x_sample, cache_k, cache_v, c, c_ctx, ada_w, ada_b, norm_mix, norm_ffn, w_in,
           hy_conv_w, hy_conv_b, hy_ffn_w1, hy_ffn_b1, hy_ffn_w2, hy_ffn_b2, hy_ffn_w3, hy_ffn_b3,
           hy_sin_freq, hy_decay, hy_skip, w_hy_out, w_fn_out, w_na_out, na_rel_bias, w_out,
           router_w, router_b, moe_w1, moe_b1, moe_w2, moe_b2, final_norm):
    Bp, Lp, _ = x_prompt.shape
    Bs, Ls, _ = x_sample.shape
    P = cache_k.shape[2]
    assert Bs + 1 <= 8 and Lp % ROW_TILE == 0 and Ls % MERGE_TILE == 0 and Ls % GRID_W == 0
    assert (Bp * Lp) % MERGE_TILE == 0 and Bp % 2 == 0 and Bs % 2 == 0
    n_p, n_s = Bp * Lp, Bs * Ls
    hyp = dict(hy_conv_w=hy_conv_w, hy_conv_b=hy_conv_b, hy_ffn_w1=hy_ffn_w1, hy_ffn_b1=hy_ffn_b1,
               hy_ffn_w2=hy_ffn_w2, hy_ffn_b2=hy_ffn_b2, hy_ffn_w3=hy_ffn_w3, hy_ffn_b3=hy_ffn_b3,
               hy_sin_freq=hy_sin_freq, hy_decay=hy_decay, hy_skip=hy_skip)

    r3 = lambda a: a.reshape(DEPTH, 1, a.shape[-1])
    wts = dict(
        norm_mix=r3(norm_mix), norm_ffn=r3(norm_ffn),
        w_g=w_in[:, :, BR_W:].astype(BF16), w_hy_out=w_hy_out.astype(BF16),
        w_fn_out=w_fn_out.astype(BF16), w_na_out=w_na_out.astype(BF16), w_out=w_out.astype(BF16),
        router_w=router_w, router_b=r3(router_b),
        w1=moe_w1,
        b1g=moe_b1[..., 0::2].reshape(DEPTH, N_EXP, 1, FF),
        b1l=moe_b1[..., 1::2].reshape(DEPTH, N_EXP, 1, FF),
        w2=moe_w2, b2=moe_b2.reshape(DEPTH, N_EXP, 1, D),
    )
    w_in_b = w_in[:, :, :BR_W].astype(BF16)

    cond8 = jnp.zeros((8, D), F32).at[0].set(c_ctx).at[1:1 + Bs].set(c)
    mod = _mod_call(cond8, ada_w, ada_b)

    streams = []
    for (B, L, base, rpc) in ((Bp, Lp, 0, Bp * Lp), (Bs, Ls, 1, Ls)):
        hy_tabs = _hyena_tables(L)
        hc, hs = _hyena_filters(L, hy_tabs, hyp)
        streams.append(dict(B=B, L=L, base=base, rpc=rpc, hy_tabs=hy_tabs, hc=hc, hs=hs,
                            fn_tabs=_fnet_tables(L)))
    rows_s = Ls // GRID_W
    assert rows_s % NBR_GROUP == 0 and rows_s >= WIN_R + NBR_GROUP - 1
    bias_tab, band_rows = _nbr_bias_table(na_rel_bias, rows_s)
    ck = cache_k.reshape(Bs, DEPTH, P, NA_W)
    cv = cache_v.reshape(Bs, DEPTH, P, NA_W)

    n_assign = (n_p + n_s) * TOP_K
    n_blocks = -(-n_assign // MOE_ROWS) + N_EXP
    n_slots = n_blocks * MOE_ROWS

    xs = jnp.zeros((n_slots * 8, 128), F32)
    xs_tok = [x_prompt.reshape(n_p, D), x_sample.reshape(n_s, D)]
    new_k, new_v = [], []
    for l in range(DEPTH):
        merged = []
        cnt = jnp.zeros((1, N_EXP), F32)
        for si, st in enumerate(streams):
            B, L = st["B"], st["L"]
            x = xs_tok[si]
            u_hy, u_fn, q, k, v = _inproj_call(l, x, mod, wts["norm_mix"], w_in_b, st["base"], st["rpc"])
            y_hy = _hyena_call(l, u_hy, B, L, st["hy_tabs"], st["hc"], st["hs"], hyp)
            y_fn = _fnet_call(u_fn, B, L, st["fn_tabs"])
            if si == 0:
                y_na = _ctx_attn_call(q, k, v, B, L)
                new_k.append(k.reshape(B, L, NA_H, NA_D))
                new_v.append(v.reshape(B, L, NA_H, NA_D))
            else:
                y_na = _nbr_attn_call(l, q, k, v, ck, cv, bias_tab, band_rows, B, L)
            xn, h2, idx, gate, rank, cnt = _merge_call(l, x, y_hy, y_fn, y_na, mod, wts, cnt,
                                                       st["base"], st["rpc"])
            merged.append((xn, h2, idx, gate, rank))

        counts = cnt[0].astype(I32)
        blocks_per_e = (counts + MOE_ROWS - 1) // MOE_ROWS
        block_end = jnp.cumsum(blocks_per_e)
        block_start = block_end - blocks_per_e
        block_expert = jnp.minimum(
            jnp.sum(jnp.arange(n_blocks, dtype=I32)[:, None] >= block_end[None, :], axis=1),
            N_EXP - 1).astype(I32)
        n_used = block_end[-1:].astype(I32)

        slots = []
        for (xn, h2, idx, gate, rank) in merged:
            start_of = jnp.sum(jnp.where(idx[..., None] == jnp.arange(N_EXP, dtype=I32), block_start, 0), axis=-1)
            slot = start_of * MOE_ROWS + rank
            slots.append(slot)
            xs = _dispatch_call(slot, h2, xs)
        ys = _moe_call(l, xs, block_expert, n_used, wts)
        for si, st in enumerate(streams):
            xn, h2, idx, gate, rank = merged[si]
            xs_tok[si] = _combine_call(l, slots[si], xn, gate, mod, final_norm, ys,
                                       st["base"], st["rpc"], l == DEPTH - 1)

    y_prompt = xs_tok[0].reshape(Bp, Lp, D)
    y_sample = xs_tok[1].reshape(Bs, Ls, D)
    return (y_prompt, y_sample, jnp.stack(new_k, axis=1), jnp.stack(new_v, axis=1))
```
